```python
import math
import jax, jax.numpy as jnp
from jax import lax
import numpy as np

D_MODEL = 1024
BATCH = 2
SEQ = 16384
DEPTH = 4
DEC_BATCH = 8
DEC_SEQ = 2048
PAST_LEN = 128

N_EVEN = (DEPTH + 1) // 2
N_ODD = DEPTH // 2
D_FF = 2816
ROPE_THETA = 10000.0
NORM_EPS = 1e-6
Q_BLOCK = 128
CONV_WIDTH = D_MODEL // 2
CONV_K = 3
DIFF_WIDTH = D_MODEL // 2
DIFF_HEADS = 4
DIFF_HEAD_DIM = DIFF_WIDTH // DIFF_HEADS // 2
EVEN_IN = 3 * CONV_WIDTH + 3 * DIFF_WIDTH
MLA_HEADS = 8
MLA_NOPE = 128
MLA_ROPE = 64
MLA_V = 128
MLA_QK = MLA_NOPE + MLA_ROPE
MLA_Q_RANK = 384
MLA_KV_RANK = 256
MLA_DOWN = MLA_Q_RANK + MLA_KV_RANK + MLA_ROPE

kernel_name = 'hybrid_conv_diffattn_mla_macaron_encoder'


def _rms_norm(x, g):
    x32 = x.astype(jnp.float32)
    y = x32 * lax.rsqrt(jnp.mean(x32 * x32, axis=-1, keepdims=True) + NORM_EPS)
    return (y * g.astype(jnp.float32)).astype(x.dtype)


def _rope(x):
    s, d = x.shape[1], x.shape[-1]
    inv = 1.0 / (ROPE_THETA ** (jnp.arange(0, d, 2, dtype=jnp.float32) / d))
    ang = jnp.arange(s, dtype=jnp.float32)[:, None] * inv[None, :]
    shape = (s,) + (1,) * (x.ndim - 3) + (d // 2,)
    cos = jnp.cos(ang).reshape(shape)
    sin = jnp.sin(ang).reshape(shape)
    x32 = x.astype(jnp.float32)
    x1, x2 = x32[..., : d // 2], x32[..., d // 2:]
    return jnp.concatenate([x1 * cos - x2 * sin, x1 * sin + x2 * cos], axis=-1).astype(x.dtype)


def _swiglu(h, w_in, w_out):
    g, u = jnp.split(h @ w_in, 2, axis=-1)
    return (jax.nn.silu(g) * u) @ w_out


def _blocks(t):
    b, s = t.shape[0], t.shape[1]
    return jnp.swapaxes(t.reshape((b, s // Q_BLOCK, Q_BLOCK) + t.shape[2:]), 0, 1)


def _unblocks(t):
    nb, b, qb = t.shape[0], t.shape[1], t.shape[2]
    return jnp.swapaxes(t, 0, 1).reshape((b, nb * qb) + t.shape[3:])


def _softmax_attention(q, k, v, scale):
    def one_block(qb):
        s = jnp.einsum('bqhd,bkhd->bhqk', qb, k).astype(jnp.float32) * scale
        p = jax.nn.softmax(s, axis=-1).astype(v.dtype)
        return jnp.einsum('bhqk,bkhe->bqhe', p, v)
    return _unblocks(lax.map(one_block, _blocks(q)))


def _diff_attention(q, k, v, lam, scale):
    def one_block(qb):
        s = jnp.einsum('bqhcd,bkhcd->bchqk', qb, k).astype(jnp.float32) * scale
        p = jax.nn.softmax(s, axis=-1)
        w = (p[:, 0] - lam * p[:, 1]).astype(v.dtype)
        return jnp.einsum('bhqk,bkhe->bqhe', w, v)
    return _unblocks(lax.map(one_block, _blocks(q)))


def _conv_diff_mixer(h, w_in, conv_w, qn_g, kn_g, lam_vecs, subln_g, w_out, lambda_init):
    b, s, _ = h.shape
    proj = h @ w_in
    c1, c2, c3 = CONV_WIDTH, 2 * CONV_WIDTH, 3 * CONV_WIDTH
    g_b, g_c, u, q, k, v = jnp.split(proj, [c1, c2, c3, c3 + DIFF_WIDTH, c3 + 2 * DIFF_WIDTH], axis=-1)
    z = jnp.pad(g_c * u, ((0, 0), (1, 1), (0, 0)))
    conv = conv_w[0] * z[:, :-2] + conv_w[1] * z[:, 1:-1] + conv_w[2] * z[:, 2:]
    y_a = g_b * conv
    q = _rope(_rms_norm(q.reshape(b, s, DIFF_HEADS, 2, DIFF_HEAD_DIM), qn_g))
    k = _rope(_rms_norm(k.reshape(b, s, DIFF_HEADS, 2, DIFF_HEAD_DIM), kn_g))
    v = v.reshape(b, s, DIFF_HEADS, 2 * DIFF_HEAD_DIM)
    lv = lam_vecs.astype(jnp.float32)
    lam = jnp.exp(jnp.sum(lv[0] * lv[1])) - jnp.exp(jnp.sum(lv[2] * lv[3])) + lambda_init
    o = _diff_attention(q, k, v, lam, DIFF_HEAD_DIM ** -0.5)
    o = _rms_norm(o, subln_g) * (1.0 - lambda_init)
    y_b = o.reshape(b, s, DIFF_WIDTH)
    return jnp.concatenate([y_a, y_b], axis=-1) @ w_out


def _mla_mixer(h, w_down, q_lat_g, kv_lat_g, w_uq, w_ukv, qn_g, kn_g, w_o):
    b, s, _ = h.shape
    lat = h @ w_down
    c_q, c_kv, k_rope = jnp.split(lat, [MLA_Q_RANK, MLA_Q_RANK + MLA_KV_RANK], axis=-1)
    q = (_rms_norm(c_q, q_lat_g) @ w_uq).reshape(b, s, MLA_HEADS, MLA_QK)
    kv = (_rms_norm(c_kv, kv_lat_g) @ w_ukv).reshape(b, s, MLA_HEADS, MLA_NOPE + MLA_V)
    k_nope, v = kv[..., :MLA_NOPE], kv[..., MLA_NOPE:]
    k_rope = jnp.broadcast_to(k_rope[:, :, None, :], (b, s, MLA_HEADS, MLA_ROPE))
    k = jnp.concatenate([k_nope, k_rope], axis=-1)
    q = _rms_norm(q, qn_g)
    k = _rms_norm(k, kn_g)
    q = jnp.concatenate([q[..., :MLA_NOPE], _rope(q[..., MLA_NOPE:])], axis=-1)
    k = jnp.concatenate([k[..., :MLA_NOPE], _rope(k[..., MLA_NOPE:])], axis=-1)
    o = _softmax_attention(q, k, v, MLA_QK ** -0.5)
    return o.reshape(b, s, MLA_HEADS * MLA_V) @ w_o


def _trunk(x, ffn1_norm, ffn1_w_in, ffn1_w_out, mix_norm, ffn2_norm, ffn2_w_in, ffn2_w_out,
           even_w_in, even_conv_w, even_q_norm, even_k_norm, even_lambda, even_subln, even_w_out,
           mla_w_down, mla_q_lat_norm, mla_kv_lat_norm, mla_w_uq, mla_w_ukv, mla_q_norm, mla_k_norm, mla_w_o):
    for l in range(DEPTH):
        x = x + 0.5 * _swiglu(_rms_norm(x, ffn1_norm[l]), ffn1_w_in[l], ffn1_w_out[l])
        h = _rms_norm(x, mix_norm[l])
        i = l // 2
        if l % 2 == 0:
            lambda_init = 0.8 - 0.6 * math.exp(-0.3 * l)
            x = x + _conv_diff_mixer(h, even_w_in[i], even_conv_w[i], even_q_norm[i], even_k_norm[i],
                                     even_lambda[i], even_subln[i], even_w_out[i], lambda_init)
        else:
            x = x + _mla_mixer(h, mla_w_down[i], mla_q_lat_norm[i], mla_kv_lat_norm[i], mla_w_uq[i],
                               mla_w_ukv[i], mla_q_norm[i], mla_k_norm[i], mla_w_o[i])
        x = x + 0.5 * _swiglu(_rms_norm(x, ffn2_norm[l]), ffn2_w_in[l], ffn2_w_out[l])
    return x


def setup_inputs(seed: int = 0) -> dict:
    key = jax.random.key(seed)
    ks = iter(jax.random.split(key, 32))

    def w(shape, fan_in):
        return jax.random.normal(next(ks), shape, jnp.float32) * (fan_in ** -0.5)

    def gain(shape):
        return 1.0 + 0.02 * jax.random.normal(next(ks), shape, jnp.float32)

    return {
        'x_prompt': jax.random.normal(next(ks), (BATCH, SEQ, D_MODEL), jnp.float32),
        'x_sample': jax.random.normal(next(ks), (DEC_BATCH, DEC_SEQ, D_MODEL), jnp.float32),
        'ffn1_norm': gain((DEPTH, D_MODEL)),
        'ffn1_w_in': w((DEPTH, D_MODEL, 2 * D_FF), D_MODEL),
        'ffn1_w_out': w((DEPTH, D_FF, D_MODEL), D_FF),
        'mix_norm': gain((DEPTH, D_MODEL)),
        'ffn2_norm': gain((DEPTH, D_MODEL)),
        'ffn2_w_in': w((DEPTH, D_MODEL, 2 * D_FF), D_MODEL),
        'ffn2_w_out': w((DEPTH, D_FF, D_MODEL), D_FF),
        'even_w_in': w((N_EVEN, D_MODEL, EVEN_IN), D_MODEL),
        'even_conv_w': w((N_EVEN, CONV_K, CONV_WIDTH), CONV_K),
        'even_q_norm': gain((N_EVEN, DIFF_HEAD_DIM)),
        'even_k_norm': gain((N_EVEN, DIFF_HEAD_DIM)),
        'even_lambda': 0.1 * jax.random.normal(next(ks), (N_EVEN, 4, DIFF_HEAD_DIM), jnp.float32),
        'even_subln': gain((N_EVEN, 2 * DIFF_HEAD_DIM)),
        'even_w_out': w((N_EVEN, CONV_WIDTH + DIFF_WIDTH, D_MODEL), CONV_WIDTH + DIFF_WIDTH),
        'mla_w_down': w((N_ODD, D_MODEL, MLA_DOWN), D_MODEL),
        'mla_q_lat_norm': gain((N_ODD, MLA_Q_RANK)),
        'mla_kv_lat_norm': gain((N_ODD, MLA_KV_RANK)),
        'mla_w_uq': w((N_ODD, MLA_Q_RANK, MLA_HEADS * MLA_QK), MLA_Q_RANK),
        'mla_w_ukv': w((N_ODD, MLA_KV_RANK, MLA_HEADS * (MLA_NOPE + MLA_V)), MLA_KV_RANK),
        'mla_q_norm': gain((N_ODD, MLA_QK)),
        'mla_k_norm': gain((N_ODD, MLA_QK)),
        'mla_w_o': w((N_ODD, MLA_HEADS * MLA_V, D_MODEL), MLA_HEADS * MLA_V),
    }


def reference(x_prompt, x_sample, ffn1_norm, ffn1_w_in, ffn1_w_out, mix_norm, ffn2_norm, ffn2_w_in, ffn2_w_out,
              even_w_in, even_conv_w, even_q_norm, even_k_norm, even_lambda, even_subln, even_w_out,
              mla_w_down, mla_q_lat_norm, mla_kv_lat_norm, mla_w_uq, mla_w_ukv, mla_q_norm, mla_k_norm, mla_w_o):
    y_prompt = _trunk(x_prompt, ffn1_norm, ffn1_w_in, ffn1_w_out, mix_norm, ffn2_norm, ffn2_w_in, ffn2_w_out,
                      even_w_in, even_conv_w, even_q_norm, even_k_norm, even_lambda, even_subln, even_w_out,
                      mla_w_down, mla_q_lat_norm, mla_kv_lat_norm, mla_w_uq, mla_w_ukv, mla_q_norm, mla_k_norm, mla_w_o)
    y_sample = _trunk(x_sample, ffn1_norm, ffn1_w_in, ffn1_w_out, mix_norm, ffn2_norm, ffn2_w_in, ffn2_w_out,
                      even_w_in, even_conv_w, even_q_norm, even_k_norm, even_lambda, even_subln, even_w_out,
                      mla_w_down, mla_q_lat_norm, mla_kv_lat_norm, mla_w_uq, mla_w_ukv, mla_q_norm, mla_k_norm, mla_w_o)
    return (y_prompt, y_sample)
```

```python
import functools
import math

import jax
import jax.numpy as jnp
from jax import lax
from jax.experimental import pallas as pl
from jax.experimental.pallas import tpu as pltpu

F32 = jnp.float32
BF16 = jnp.bfloat16

NORM_EPS = 1e-6
ROPE_THETA = 10000.0

D_MODEL = 1024
D_FF = 2816
CONV_WIDTH = 512
DIFF_WIDTH = 512
DIFF_HEADS = 4
DIFF_HEAD_DIM = 64
EVEN_IN = 3 * CONV_WIDTH + 3 * DIFF_WIDTH
MLA_HEADS = 8
MLA_NOPE = 128
MLA_ROPE = 64
MLA_V = 128
MLA_QK = MLA_NOPE + MLA_ROPE
MLA_Q_RANK = 384
MLA_KV_RANK = 256
MLA_DOWN = MLA_Q_RANK + MLA_KV_RANK + MLA_ROPE
MLA_DOWN_PAD = 768

LANES = 128
HALO_ROWS = 8
VMEM_LIMIT = 56 * 1024 * 1024

TILES = dict(ffn=256, tok=512, tq=512, tkb=2048, tkc=512)


def _tile(name, n):
    t = min(TILES[name], n)
    assert n % t == 0, (name, n, t)
    return t


def _params(*sem):
    return pltpu.CompilerParams(dimension_semantics=sem, vmem_limit_bytes=VMEM_LIMIT)


def _const_spec(shape):
    nd = len(shape)
    return pl.BlockSpec(shape, lambda *_: (0,) * nd, pipeline_mode=pl.Buffered(1))


def _rms(x, g):
    ms = jnp.mean(x * x, axis=-1, keepdims=True)
    return x * lax.rsqrt(ms + NORM_EPS) * g


def _dot(a, b):
    return jnp.dot(a, b, preferred_element_type=F32)


def _ffn_kernel(x_ref, g_ref, win_ref, wout_ref, o_ref):
    x = x_ref[...]
    xn = _rms(x, g_ref[...]).astype(BF16)
    h = _dot(xn, win_ref[...])
    gate, up = h[:, :D_FF], h[:, D_FF:]
    act = (gate / (1.0 + jnp.exp(-gate)) * up).astype(BF16)
    o_ref[...] = x + 0.5 * _dot(act, wout_ref[...])


def _ffn(x, g, w_in, w_out):
    t, d = x.shape
    tm = _tile('ffn', t)
    return pl.pallas_call(
        _ffn_kernel,
        grid=(t // tm,),
        in_specs=[pl.BlockSpec((tm, d), lambda i: (i, 0)),
                  _const_spec((1, d)),
                  _const_spec(w_in.shape),
                  _const_spec(w_out.shape)],
        out_specs=pl.BlockSpec((tm, d), lambda i: (i, 0)),
        out_shape=jax.ShapeDtypeStruct((t, d), F32),
        compiler_params=_params("parallel"),
        name="ffn",
    )(x, g, w_in, w_out)


def _seg64_rsqrt(x):
    n = x.shape[0]
    lo = lax.broadcasted_iota(jnp.int32, (n, LANES), 1) < 64
    outs = []
    for p in range(x.shape[1] // LANES):
        x2 = x[:, LANES * p:LANES * (p + 1)]
        x2 = x2 * x2
        s_lo = jnp.sum(jnp.where(lo, x2, 0.0), axis=-1, keepdims=True)
        s_hi = jnp.sum(jnp.where(lo, 0.0, x2), axis=-1, keepdims=True)
        outs.append(lax.rsqrt(jnp.where(lo, s_lo, s_hi) * (1.0 / 64) + NORM_EPS))
    return jnp.concatenate(outs, axis=-1)


def _rope_lanes(x, cos, sa, sb):
    w = x.shape[1]
    rep = w // LANES
    if rep > 1:
        cos, sa, sb = (jnp.concatenate([t] * rep, axis=1) for t in (cos, sa, sb))
    return x * cos + pltpu.roll(x, w - 32, 1) * sa + pltpu.roll(x, 32, 1) * sb


def _rope_tables(s):
    d = 64
    inv = 1.0 / (ROPE_THETA ** (jnp.arange(0, d, 2, dtype=F32) / d))
    ang = jnp.arange(s, dtype=F32)[:, None] * inv[None, :]
    cos, sin = jnp.cos(ang), jnp.sin(ang)
    z = jnp.zeros_like(sin)
    cos = jnp.concatenate([cos, cos] * 2, axis=1)
    sa = jnp.concatenate([-sin, z] * 2, axis=1)
    sb = jnp.concatenate([z, sin] * 2, axis=1)
    return cos, sa, sb


def _even_in_kernel(x_ref, xp_ref, xn_ref, g_ref, w_ref, cw_ref, qg_ref, kg_ref, cos_ref, sa_ref, sb_ref,
                    ya_ref, qT_ref, k_ref, vT_ref):
    i = pl.program_id(1)
    n = pl.num_programs(1)
    ts = x_ref.shape[0]
    c1, c2, c3 = CONV_WIDTH, 2 * CONV_WIDTH, 3 * CONV_WIDTH
    g = g_ref[...]
    xn = _rms(x_ref[...], g).astype(BF16)
    proj = _dot(xn, w_ref[...])
    z = proj[:, c1:c2] * proj[:, c2:c3]
    xh = jnp.concatenate([xp_ref[...], xn_ref[...]], axis=0)
    ph = _dot(_rms(xh, g).astype(BF16), w_ref[:, c1:c3])
    zh = ph[:, :c1] * ph[:, c1:]
    z_before = jnp.where(i > 0, zh[HALO_ROWS - 1:HALO_ROWS], 0.0)
    z_after = jnp.where(i < n - 1, zh[HALO_ROWS:HALO_ROWS + 1], 0.0)
    row = lax.broadcasted_iota(jnp.int32, (ts, c1), 0)
    z_m1 = jnp.where(row == 0, z_before, pltpu.roll(z, 1, 0))
    z_p1 = jnp.where(row == ts - 1, z_after, pltpu.roll(z, ts - 1, 0))
    cw = cw_ref[...]
    conv = cw[0:1] * z_m1 + cw[1:2] * z + cw[2:3] * z_p1
    ya_ref[...] = (proj[:, :c1] * conv).astype(BF16)

    cos, sa, sb = cos_ref[...], sa_ref[...], sb_ref[...]
    q = proj[:, c3:c3 + DIFF_WIDTH]
    q = _rope_lanes(q * _seg64_rsqrt(q) * qg_ref[...], cos, sa, sb) * (DIFF_HEAD_DIM ** -0.5)
    qT_ref[...] = q.T.astype(BF16)
    k = proj[:, c3 + DIFF_WIDTH:c3 + 2 * DIFF_WIDTH]
    k_ref[...] = _rope_lanes(k * _seg64_rsqrt(k) * kg_ref[...], cos, sa, sb).astype(BF16)
    vT_ref[...] = proj[:, c3 + 2 * DIFF_WIDTH:].T.astype(BF16)


def _even_in(x, g, w_in, conv_w, qg, kg, tables):
    b, s, d = x.shape
    ts = _tile('tok', s)
    hb = ts // HALO_ROWS
    nhb = s // HALO_ROWS
    tok = lambda w: pl.BlockSpec((None, ts, w), lambda bi, i: (bi, i, 0))
    tokT = lambda w: pl.BlockSpec((None, w, ts), lambda bi, i: (bi, 0, i))
    tab = pl.BlockSpec((ts, LANES), lambda bi, i: (i, 0))
    return pl.pallas_call(
        _even_in_kernel,
        grid=(b, s // ts),
        in_specs=[tok(d),
                  pl.BlockSpec((None, HALO_ROWS, d), lambda bi, i: (bi, jnp.maximum(i * hb - 1, 0), 0)),
                  pl.BlockSpec((None, HALO_ROWS, d), lambda bi, i: (bi, jnp.minimum((i + 1) * hb, nhb - 1), 0)),
                  _const_spec((1, d)), _const_spec(w_in.shape), _const_spec(conv_w.shape),
                  _const_spec(qg.shape), _const_spec(kg.shape), tab, tab, tab],
        out_specs=[tok(CONV_WIDTH), tokT(DIFF_WIDTH), tok(DIFF_WIDTH), tokT(DIFF_WIDTH)],
        out_shape=[jax.ShapeDtypeStruct((b, s, CONV_WIDTH), BF16),
                   jax.ShapeDtypeStruct((b, DIFF_WIDTH, s), BF16),
                   jax.ShapeDtypeStruct((b, s, DIFF_WIDTH), BF16),
                   jax.ShapeDtypeStruct((b, DIFF_WIDTH, s), BF16)],
        compiler_params=_params("parallel", "parallel"),
        name="even_in",
    )(x, x, x, g, w_in, conv_w, qg, kg, *tables)


def _flash_chunks(qmat, k_ref, vT_ref, m_ref, l_ref, acc_ref, tkc):
    for c in range(k_ref.shape[0] // tkc):
        s = _dot(k_ref[c * tkc:(c + 1) * tkc, :], qmat)
        m_prev = m_ref[...]
        m_new = jnp.maximum(m_prev, jnp.max(s, axis=0, keepdims=True))
        alpha = jnp.exp(m_prev - m_new)
        p = jnp.exp(s - m_new)
        l_ref[...] = alpha * l_ref[...] + jnp.sum(p, axis=0, keepdims=True)
        acc_ref[...] = alpha * acc_ref[...] + _dot(vT_ref[:, c * tkc:(c + 1) * tkc], p.astype(BF16))
        m_ref[...] = m_new


def _flash_init(m_ref, l_ref, acc_ref):
    m_ref[...] = jnp.full(m_ref.shape, -jnp.inf, F32)
    l_ref[...] = jnp.zeros(l_ref.shape, F32)
    acc_ref[...] = jnp.zeros(acc_ref.shape, F32)


def _diff_attn_kernel(qT_ref, k_ref, vT_ref, lam_ref, sg_ref, o_ref, qbd_ref, m_ref, l_ref, acc_ref,
                      *, tkc, lambda_init):
    j = pl.program_id(3)
    tq = qT_ref.shape[1]
    dh = DIFF_HEAD_DIM

    @pl.when(j == 0)
    def _():
        _flash_init(m_ref, l_ref, acc_ref)
        qbd_ref[...] = jnp.zeros(qbd_ref.shape, BF16)
        qbd_ref[0:dh, 0:tq] = qT_ref[0:dh, :]
        qbd_ref[dh:2 * dh, tq:2 * tq] = qT_ref[dh:2 * dh, :]

    _flash_chunks(qbd_ref[...], k_ref, vT_ref, m_ref, l_ref, acc_ref, tkc)

    @pl.when(j == pl.num_programs(3) - 1)
    def _():
        o = acc_ref[...] / l_ref[...]
        lv = lam_ref[...]
        lam = (jnp.exp(jnp.sum(lv[0:1] * lv[1:2], axis=-1, keepdims=True))
               - jnp.exp(jnp.sum(lv[2:3] * lv[3:4], axis=-1, keepdims=True)) + lambda_init)
        od = o[:, :tq] - lam * o[:, tq:]
        ms = jnp.mean(od * od, axis=0, keepdims=True)
        on = od * lax.rsqrt(ms + NORM_EPS) * sg_ref[...] * (1.0 - lambda_init)
        o_ref[...] = on.T.astype(o_ref.dtype)


def _diff_attn(qT, k, vT, lam_vecs, subln_g, lambda_init):
    b, w, s = qT.shape
    tq, tkb = _tile('tq', s), _tile('tkb', s)
    tkc = _tile('tkc', tkb)
    hw = 2 * DIFF_HEAD_DIM
    return pl.pallas_call(
        functools.partial(_diff_attn_kernel, tkc=tkc, lambda_init=lambda_init),
        grid=(b, DIFF_HEADS, s // tq, s // tkb),
        in_specs=[pl.BlockSpec((None, hw, tq), lambda bi, h, i, j: (bi, h, i)),
                  pl.BlockSpec((None, tkb, hw), lambda bi, h, i, j: (bi, j, h)),
                  pl.BlockSpec((None, hw, tkb), lambda bi, h, i, j: (bi, h, j)),
                  _const_spec(lam_vecs.shape), _const_spec(subln_g.shape)],
        out_specs=pl.BlockSpec((None, tq, hw), lambda bi, h, i, j: (bi, i, h)),
        out_shape=jax.ShapeDtypeStruct((b, s, w), BF16),
        scratch_shapes=[pltpu.VMEM((hw, 2 * tq), BF16), pltpu.VMEM((1, 2 * tq), F32),
                        pltpu.VMEM((1, 2 * tq), F32), pltpu.VMEM((hw, 2 * tq), F32)],
        compiler_params=_params("parallel", "parallel", "parallel", "arbitrary"),
        name="diff_attn",
    )(qT, k, vT, lam_vecs, subln_g)


def _mla_attn_kernel(qT_ref, k_ref, vT_ref, o_ref, m_ref, l_ref, acc_ref, *, tkc):
    j = pl.program_id(3)

    @pl.when(j == 0)
    def _():
        _flash_init(m_ref, l_ref, acc_ref)

    _flash_chunks(qT_ref[...], k_ref, vT_ref, m_ref, l_ref, acc_ref, tkc)

    @pl.when(j == pl.num_programs(3) - 1)
    def _():
        o_ref[...] = (acc_ref[...] / l_ref[...]).T.astype(o_ref.dtype)


def _mla_attn(qT, k, vT):
    b, nh, dk, s = qT.shape
    tq, tkb = _tile('tq', s), _tile('tkb', s)
    tkc = _tile('tkc', tkb)
    return pl.pallas_call(
        functools.partial(_mla_attn_kernel, tkc=tkc),
        grid=(b, nh, s // tq, s // tkb),
        in_specs=[pl.BlockSpec((None, None, dk, tq), lambda bi, h, i, j: (bi, h, 0, i)),
                  pl.BlockSpec((None, None, tkb, dk), lambda bi, h, i, j: (bi, h, j, 0)),
                  pl.BlockSpec((None, MLA_V, tkb), lambda bi, h, i, j: (bi, h, j))],
        out_specs=pl.BlockSpec((None, tq, MLA_V), lambda bi, h, i, j: (bi, i, h)),
        out_shape=jax.ShapeDtypeStruct((b, s, nh * MLA_V), BF16),
        scratch_shapes=[pltpu.VMEM((1, tq), F32), pltpu.VMEM((1, tq), F32), pltpu.VMEM((MLA_V, tq), F32)],
        compiler_params=_params("parallel", "parallel", "parallel", "arbitrary"),
        name="mla_attn",
    )(qT, k, vT)


def _out2_kernel(x_ref, a_ref, b_ref, w_ref, o_ref):
    ka = a_ref.shape[1]
    o_ref[...] = x_ref[...] + _dot(a_ref[...], w_ref[:ka, :]) + _dot(b_ref[...], w_ref[ka:, :])


def _out1_kernel(x_ref, a_ref, w_ref, o_ref):
    o_ref[...] = x_ref[...] + _dot(a_ref[...], w_ref[...])


def _out_proj(x, acts, w):
    t, d = x.shape
    tm = _tile('tok', t)
    row = lambda wd: pl.BlockSpec((tm, wd), lambda i: (i, 0))
    return pl.pallas_call(
        _out2_kernel if len(acts) == 2 else _out1_kernel,
        grid=(t // tm,),
        in_specs=[row(d)] + [row(a.shape[1]) for a in acts] + [_const_spec(w.shape)],
        out_specs=row(d),
        out_shape=jax.ShapeDtypeStruct((t, d), F32),
        compiler_params=_params("parallel"),
        name="out_proj",
    )(x, *acts, w)


def _mla_in_kernel(x_ref, g_ref, wd_ref, qlg_ref, kvlg_ref, wuq_ref, wukv_ref, qgn_ref, qgr_ref, kgn_ref, kgr_ref,
                   cos_ref, sa_ref, sb_ref, qT_ref, k_ref, vT_ref):
    nh, dn, dr = MLA_HEADS, MLA_NOPE, MLA_ROPE
    ts = x_ref.shape[0]
    cos, sa, sb = cos_ref[...], sa_ref[...], sb_ref[...]
    xn = _rms(x_ref[...], g_ref[...]).astype(BF16)
    lat = _dot(xn, wd_ref[...])
    cq = _rms(lat[:, :MLA_Q_RANK], qlg_ref[...]).astype(BF16)
    ckv = _rms(lat[:, MLA_Q_RANK:MLA_Q_RANK + MLA_KV_RANK], kvlg_ref[...]).astype(BF16)
    kr = lat[:, MLA_Q_RANK + MLA_KV_RANK:]
    q = _dot(cq, wuq_ref[...])
    kv = _dot(ckv, wukv_ref[...])
    lo = lax.broadcasted_iota(jnp.int32, (ts, LANES), 1) < dr
    inv_d = 1.0 / MLA_QK
    scale = MLA_QK ** -0.5

    def ssq(a):
        return jnp.sum(a * a, axis=-1, keepdims=True)

    qn_parts, qr_scale = [], []
    for pair in range(nh // 2):
        qr2 = q[:, nh * dn + LANES * pair:nh * dn + LANES * (pair + 1)]
        qr2 = qr2 * qr2
        s_even = jnp.sum(jnp.where(lo, qr2, 0.0), axis=-1, keepdims=True)
        s_odd = jnp.sum(jnp.where(lo, 0.0, qr2), axis=-1, keepdims=True)
        rs = []
        for h, s_r in ((2 * pair, s_even), (2 * pair + 1, s_odd)):
            qn_h = q[:, h * dn:(h + 1) * dn]
            r = lax.rsqrt((ssq(qn_h) + s_r) * inv_d + NORM_EPS)
            qn_parts.append(qn_h * r)
            rs.append(r)
        qr_scale.append(jnp.where(lo, rs[0], rs[1]))
    qn = jnp.concatenate(qn_parts, axis=1) * qgn_ref[...] * scale
    qr = q[:, nh * dn:] * jnp.concatenate(qr_scale, axis=1) * qgr_ref[...]
    qr = _rope_lanes(qr, cos, sa, sb) * scale
    qnT = qn.T.astype(BF16)
    qrT = qr.T.astype(BF16)
    for h in range(nh):
        qT_ref[h, 0:dn, :] = qnT[h * dn:(h + 1) * dn, :]
        qT_ref[h, dn:dn + dr, :] = qrT[h * dr:(h + 1) * dr, :]

    kr_ssq = ssq(kr)
    kr_rot = _rope_lanes(kr * kgr_ref[...], cos, sa, sb)
    kgn = kgn_ref[...]
    for h in range(nh):
        kn_h = kv[:, h * dn:(h + 1) * dn]
        r = lax.rsqrt((ssq(kn_h) + kr_ssq) * inv_d + NORM_EPS)
        k_ref[h, :, 0:dn] = (kn_h * r * kgn).astype(BF16)
        k_ref[h, :, dn:dn + dr] = (kr_rot * r)[:, :dr].astype(BF16)
    vT_ref[...] = kv[:, nh * dn:].T.astype(BF16)


def _mla_in(x, g, wd, qlg, kvlg, wuq, wukv, qgn, qgr, kgn, kgr, tables):
    b, s, d = x.shape
    ts = _tile('tok', s)
    nh = MLA_HEADS
    tab = pl.BlockSpec((ts, LANES), lambda bi, i: (i, 0))
    consts = [g, wd, qlg, kvlg, wuq, wukv, qgn, qgr, kgn, kgr]
    return pl.pallas_call(
        _mla_in_kernel,
        grid=(b, s // ts),
        in_specs=[pl.BlockSpec((None, ts, d), lambda bi, i: (bi, i, 0))]
                 + [_const_spec(c.shape) for c in consts] + [tab, tab, tab],
        out_specs=[pl.BlockSpec((None, nh, MLA_QK, ts), lambda bi, i: (bi, 0, 0, i)),
                   pl.BlockSpec((None, nh, ts, MLA_QK), lambda bi, i: (bi, 0, i, 0)),
                   pl.BlockSpec((None, nh * MLA_V, ts), lambda bi, i: (bi, 0, i))],
        out_shape=[jax.ShapeDtypeStruct((b, nh, MLA_QK, s), BF16),
                   jax.ShapeDtypeStruct((b, nh, s, MLA_QK), BF16),
                   jax.ShapeDtypeStruct((b, nh * MLA_V, s), BF16)],
        compiler_params=_params("parallel", "parallel"),
        name="mla_in",
    )(x, *consts, *tables)


def _prep_weights(ffn1_norm, ffn1_w_in, ffn1_w_out, mix_norm, ffn2_norm, ffn2_w_in, ffn2_w_out,
                  even_w_in, even_conv_w, even_q_norm, even_k_norm, even_lambda, even_subln, even_w_out,
                  mla_w_down, mla_q_lat_norm, mla_kv_lat_norm, mla_w_uq, mla_w_ukv, mla_q_norm, mla_k_norm, mla_w_o):
    nh, dn, dr = MLA_HEADS, MLA_NOPE, MLA_ROPE
    n_odd = mla_w_down.shape[0]
    row = lambda a: a[:, None, :]
    wuq = mla_w_uq.reshape(n_odd, MLA_Q_RANK, nh, MLA_QK)
    wukv = mla_w_ukv.reshape(n_odd, MLA_KV_RANK, nh, dn + MLA_V)
    tile = lambda a, k: jnp.tile(a, (1, k))[:, None, :]
    return dict(
        ffn1_norm=row(ffn1_norm), ffn1_w_in=ffn1_w_in.astype(BF16), ffn1_w_out=ffn1_w_out.astype(BF16),
        mix_norm=row(mix_norm),
        ffn2_norm=row(ffn2_norm), ffn2_w_in=ffn2_w_in.astype(BF16), ffn2_w_out=ffn2_w_out.astype(BF16),
        even_w_in=even_w_in.astype(BF16), even_conv_w=even_conv_w,
        even_qg=tile(even_q_norm, DIFF_WIDTH // DIFF_HEAD_DIM), even_kg=tile(even_k_norm, DIFF_WIDTH // DIFF_HEAD_DIM),
        even_lambda=even_lambda, even_subln=even_subln[:, :, None], even_w_out=even_w_out.astype(BF16),
        mla_w_down=jnp.pad(mla_w_down, ((0, 0), (0, 0), (0, MLA_DOWN_PAD - MLA_DOWN))).astype(BF16),
        mla_qlg=row(mla_q_lat_norm), mla_kvlg=row(mla_kv_lat_norm),
        mla_w_uq=jnp.concatenate([wuq[..., :dn].reshape(n_odd, MLA_Q_RANK, nh * dn),
                                  wuq[..., dn:].reshape(n_odd, MLA_Q_RANK, nh * dr)], axis=-1).astype(BF16),
        mla_w_ukv=jnp.concatenate([wukv[..., :dn].reshape(n_odd, MLA_KV_RANK, nh * dn),
                                   wukv[..., dn:].reshape(n_odd, MLA_KV_RANK, nh * MLA_V)], axis=-1).astype(BF16),
        mla_qgn=tile(mla_q_norm[:, :dn], nh), mla_qgr=tile(mla_q_norm[:, dn:], nh),
        mla_kgn=row(mla_k_norm[:, :dn]),
        mla_kgr=row(jnp.pad(mla_k_norm[:, dn:], ((0, 0), (0, LANES - dr)))),
        mla_w_o=mla_w_o.astype(BF16),
    )


def _trunk(x, p):
    b, s, d = x.shape
    depth = p['ffn1_norm'].shape[0]
    tables = _rope_tables(s)
    flat = lambda a: a.reshape(b * s, a.shape[-1])
    x = flat(x)
    for l in range(depth):
        i = l // 2
        x = _ffn(x, p['ffn1_norm'][l], p['ffn1_w_in'][l], p['ffn1_w_out'][l])
        x3 = x.reshape(b, s, d)
        if l % 2 == 0:
            lambda_init = 0.8 - 0.6 * math.exp(-0.3 * l)
            ya, qT, k, vT = _even_in(x3, p['mix_norm'][l], p['even_w_in'][i], p['even_conv_w'][i],
                                     p['even_qg'][i], p['even_kg'][i], tables)
            yb = _diff_attn(qT, k, vT, p['even_lambda'][i], p['even_subln'][i], lambda_init)
            x = _out_proj(x, [flat(ya), flat(yb)], p['even_w_out'][i])
        else:
            qT, k, vT = _mla_in(x3, p['mix_norm'][l], p['mla_w_down'][i], p['mla_qlg'][i], p['mla_kvlg'][i],
                                p['mla_w_uq'][i], p['mla_w_ukv'][i], p['mla_qgn'][i], p['mla_qgr'][i],
                                p['mla_kgn'][i], p['mla_kgr'][i], tables)
            o = _mla_attn(qT, k, vT)
            x = _out_proj(x, [flat(o)], p['mla_w_o'][i])
        x = _ffn(x, p['ffn2_norm'][l], p['ffn2_w_in'][l], p['ffn2_w_out'][l])
    return x.reshape(b, s, d)


def kernel(x_prompt, x_sample, ffn1_norm, ffn1_w_in, ffn1_w_out, mix_norm, ffn2_norm, ffn2_w_in, ffn2_w_out, even_w_in, even_conv_w, even_q_norm, even_k_norm, even_lambda, even_subln, even_w_out, mla_w_down, mla_q_lat_norm, mla_kv_lat_norm, mla_w_uq, mla_w_ukv, mla_q_norm, mla_k_norm, mla_w_o):
    p = _prep_weights(ffn1_norm, ffn1_w_in, ffn1_w_out, mix_norm, ffn2_norm, ffn2_w_in, ffn2_w_out,
                      even_w_in, even_conv_w, even_q_norm, even_k_norm, even_lambda, even_subln, even_w_out,
                      mla_w_down, mla_q_lat_norm, mla_kv_lat_norm, mla_w_uq, mla_w_ukv, mla_q_norm, mla_k_norm,
                      mla_w_o)
    return (_trunk(x_prompt, p), _trunk(x_sample, p))
```

```python
import functools
import math

import jax
import jax.numpy as jnp
from jax import lax
from jax.experimental import pallas as pl
from jax.experimental.pallas import tpu as pltpu

F32 = jnp.float32
BF16 = jnp.bfloat16

NORM_EPS = 1e-6
ROPE_THETA = 10000.0
LOG2E = math.log2(math.e)
SCORE_BOUND = 60.0

D_MODEL = 1024
D_FF = 2816
CONV_WIDTH = 512
DIFF_WIDTH = 512
DIFF_HEADS = 4
DIFF_HEAD_DIM = 64
EVEN_IN = 3 * CONV_WIDTH + 3 * DIFF_WIDTH
MLA_HEADS = 8
MLA_NOPE = 128
MLA_ROPE = 64
MLA_V = 128
MLA_QK = MLA_NOPE + MLA_ROPE
MLA_Q_RANK = 384
MLA_KV_RANK = 256
MLA_DOWN = MLA_Q_RANK + MLA_KV_RANK + MLA_ROPE
MLA_DOWN_PAD = 768

LANES = 128
HALO_ROWS = 8
VMEM_LIMIT = 56 * 1024 * 1024

TILES = dict(ffn=256, tok=512, tq=512, tq_mla=1024, tkc=512)


def _tile(name, n):
    t = min(TILES[name], n)
    assert n % t == 0, (name, n, t)
    return t


def _params(*sem):
    return pltpu.CompilerParams(dimension_semantics=sem, vmem_limit_bytes=VMEM_LIMIT)


def _const_spec(shape):
    nd = len(shape)
    return pl.BlockSpec(shape, lambda *_: (0,) * nd, pipeline_mode=pl.Buffered(1))


def _rms(x, g):
    ms = jnp.mean(x * x, axis=-1, keepdims=True)
    return x * lax.rsqrt(ms + NORM_EPS) * g


def _dot(a, b):
    return jnp.dot(a, b, preferred_element_type=F32)


def _ffn_kernel(x_ref, g_ref, win_ref, wout_ref, o_ref):
    x = x_ref[...]
    xn = _rms(x, g_ref[...]).astype(BF16)
    h = _dot(xn, win_ref[...])
    gate, up = h[:, :D_FF], h[:, D_FF:]
    act = (gate / (1.0 + jnp.exp(-gate)) * up).astype(BF16)
    o_ref[...] = x + 0.5 * _dot(act, wout_ref[...])


def _ffn(x, g, w_in, w_out):
    t, d = x.shape
    tm = _tile('ffn', t)
    return pl.pallas_call(
        _ffn_kernel,
        grid=(t // tm,),
        in_specs=[pl.BlockSpec((tm, d), lambda i: (i, 0)),
                  _const_spec((1, d)),
                  _const_spec(w_in.shape),
                  _const_spec(w_out.shape)],
        out_specs=pl.BlockSpec((tm, d), lambda i: (i, 0)),
        out_shape=jax.ShapeDtypeStruct((t, d), F32),
        compiler_params=_params("parallel"),
        name="ffn",
    )(x, g, w_in, w_out)


def _seg64_rsqrt(x):
    n = x.shape[0]
    lo = lax.broadcasted_iota(jnp.int32, (n, LANES), 1) < 64
    outs = []
    for p in range(x.shape[1] // LANES):
        x2 = x[:, LANES * p:LANES * (p + 1)]
        x2 = x2 * x2
        s_lo = jnp.sum(jnp.where(lo, x2, 0.0), axis=-1, keepdims=True)
        s_hi = jnp.sum(jnp.where(lo, 0.0, x2), axis=-1, keepdims=True)
        outs.append(lax.rsqrt(jnp.where(lo, s_lo, s_hi) * (1.0 / 64) + NORM_EPS))
    return jnp.concatenate(outs, axis=-1)


def _rope_lanes(x, cos, sa, sb):
    w = x.shape[1]
    rep = w // LANES
    if rep > 1:
        cos, sa, sb = (jnp.concatenate([t] * rep, axis=1) for t in (cos, sa, sb))
    return x * cos + pltpu.roll(x, w - 32, 1) * sa + pltpu.roll(x, 32, 1) * sb


def _rope_tables(s):
    d = 64
    inv = 1.0 / (ROPE_THETA ** (jnp.arange(0, d, 2, dtype=F32) / d))
    ang = jnp.arange(s, dtype=F32)[:, None] * inv[None, :]
    cos, sin = jnp.cos(ang), jnp.sin(ang)
    z = jnp.zeros_like(sin)
    cos = jnp.concatenate([cos, cos] * 2, axis=1)
    sa = jnp.concatenate([-sin, z] * 2, axis=1)
    sb = jnp.concatenate([z, sin] * 2, axis=1)
    return cos, sa, sb


def _even_in_kernel(x_ref, xp_ref, xn_ref, g_ref, w_ref, cw_ref, qg_ref, kg_ref, cos_ref, sa_ref, sb_ref,
                    ya_ref, qT_ref, k_ref, vT_ref):
    i = pl.program_id(1)
    n = pl.num_programs(1)
    ts = x_ref.shape[0]
    c1, c2, c3 = CONV_WIDTH, 2 * CONV_WIDTH, 3 * CONV_WIDTH
    g = g_ref[...]
    xn = _rms(x_ref[...], g).astype(BF16)
    proj = _dot(xn, w_ref[...])
    z = proj[:, c1:c2] * proj[:, c2:c3]
    xh = jnp.concatenate([xp_ref[...], xn_ref[...]], axis=0)
    ph = _dot(_rms(xh, g).astype(BF16), w_ref[:, c1:c3])
    zh = ph[:, :c1] * ph[:, c1:]
    z_before = jnp.where(i > 0, zh[HALO_ROWS - 1:HALO_ROWS], 0.0)
    z_after = jnp.where(i < n - 1, zh[HALO_ROWS:HALO_ROWS + 1], 0.0)
    row = lax.broadcasted_iota(jnp.int32, (ts, c1), 0)
    z_m1 = jnp.where(row == 0, z_before, pltpu.roll(z, 1, 0))
    z_p1 = jnp.where(row == ts - 1, z_after, pltpu.roll(z, ts - 1, 0))
    cw = cw_ref[...]
    conv = cw[0:1] * z_m1 + cw[1:2] * z + cw[2:3] * z_p1
    ya_ref[...] = (proj[:, :c1] * conv).astype(BF16)

    cos, sa, sb = cos_ref[...], sa_ref[...], sb_ref[...]
    q = proj[:, c3:c3 + DIFF_WIDTH]
    q = _rope_lanes(q * _seg64_rsqrt(q) * qg_ref[...], cos, sa, sb) * (DIFF_HEAD_DIM ** -0.5 * LOG2E)
    qT_ref[...] = q.T.astype(BF16)
    k = proj[:, c3 + DIFF_WIDTH:c3 + 2 * DIFF_WIDTH]
    k_ref[...] = _rope_lanes(k * _seg64_rsqrt(k) * kg_ref[...], cos, sa, sb).astype(BF16)
    vT_ref[...] = proj[:, c3 + 2 * DIFF_WIDTH:].T.astype(BF16)


def _even_in(x, g, w_in, conv_w, qg, kg, tables):
    b, s, d = x.shape
    ts = _tile('tok', s)
    hb = ts // HALO_ROWS
    nhb = s // HALO_ROWS
    tok = lambda w: pl.BlockSpec((None, ts, w), lambda bi, i: (bi, i, 0))
    tokT = lambda w: pl.BlockSpec((None, w, ts), lambda bi, i: (bi, 0, i))
    tab = pl.BlockSpec((ts, LANES), lambda bi, i: (i, 0))
    return pl.pallas_call(
        _even_in_kernel,
        grid=(b, s // ts),
        in_specs=[tok(d),
                  pl.BlockSpec((None, HALO_ROWS, d), lambda bi, i: (bi, jnp.maximum(i * hb - 1, 0), 0)),
                  pl.BlockSpec((None, HALO_ROWS, d), lambda bi, i: (bi, jnp.minimum((i + 1) * hb, nhb - 1), 0)),
                  _const_spec((1, d)), _const_spec(w_in.shape), _const_spec(conv_w.shape),
                  _const_spec(qg.shape), _const_spec(kg.shape), tab, tab, tab],
        out_specs=[tok(CONV_WIDTH), tokT(DIFF_WIDTH), tok(DIFF_WIDTH), tokT(DIFF_WIDTH)],
        out_shape=[jax.ShapeDtypeStruct((b, s, CONV_WIDTH), BF16),
                   jax.ShapeDtypeStruct((b, DIFF_WIDTH, s), BF16),
                   jax.ShapeDtypeStruct((b, s, DIFF_WIDTH), BF16),
                   jax.ShapeDtypeStruct((b, DIFF_WIDTH, s), BF16)],
        compiler_params=_params("parallel", "parallel"),
        name="even_in",
    )(x, x, x, g, w_in, conv_w, qg, kg, *tables)


def _flash_sweep(qmat, k_ref, vT_ref, tkc):
    nk = k_ref.shape[0] // tkc
    w = qmat.shape[1]
    dv = vT_ref.shape[0]

    def scores(c):
        off = pl.multiple_of(c * tkc, tkc)
        return _dot(k_ref[pl.ds(off, tkc), :], qmat)

    def consume(c, m, l, acc, s):
        off = pl.multiple_of(c * tkc, tkc)
        m_new = jnp.maximum(m, jnp.max(s, axis=0, keepdims=True))
        alpha = jnp.exp2(m - m_new)
        p = jnp.exp2(s - m_new)
        l = alpha * l + jnp.sum(p, axis=0, keepdims=True)
        acc = alpha * acc + _dot(vT_ref[:, pl.ds(off, tkc)], p.astype(BF16))
        return m_new, l, acc

    def step(c, carry):
        m, l, acc, s = carry
        s_next = scores(c + 1)
        return consume(c, m, l, acc, s) + (s_next,)

    init = (jnp.full((1, w), -jnp.inf, F32), jnp.zeros((1, w), F32), jnp.zeros((dv, w), F32), scores(0))
    m, l, acc, s = lax.fori_loop(0, nk - 1, step, init)
    _, l, acc = consume(nk - 1, m, l, acc, s)
    return l, acc


def _bounded_sweep(qmat, k_ref, vT_ref, l_ref, acc_ref, tkc):
    nk = k_ref.shape[0] // tkc
    l_ref[...] = jnp.zeros(l_ref.shape, F32)
    acc_ref[...] = jnp.zeros(acc_ref.shape, F32)

    def step(c, carry):
        off = pl.multiple_of(c * tkc, tkc)
        p = jnp.exp2(_dot(k_ref[pl.ds(off, tkc), :], qmat))
        l_ref[...] += jnp.sum(p, axis=0, keepdims=True)
        acc_ref[...] += _dot(vT_ref[:, pl.ds(off, tkc)], p.astype(BF16))
        return carry

    lax.fori_loop(0, nk, step, 0, unroll=2)
    return l_ref[...], acc_ref[...]


def _sweep(bounded, qmat, k_ref, vT_ref, l_ref, acc_ref, tkc):
    if bounded:
        return _bounded_sweep(qmat, k_ref, vT_ref, l_ref, acc_ref, tkc)
    return _flash_sweep(qmat, k_ref, vT_ref, tkc)


def _score_bound(d, scale, qg, kg):
    margin = 1.02
    return margin * scale * LOG2E * d * jnp.max(jnp.abs(qg)) * jnp.max(jnp.abs(kg))


def _diff_attn_kernel(qT_ref, k_ref, vT_ref, lam_ref, sg_ref, o_ref, qbd_ref, l_ref, acc_ref,
                      *, tkc, lambda_init, bounded):
    tq = qT_ref.shape[1]
    dh = DIFF_HEAD_DIM
    qbd_ref[...] = jnp.zeros(qbd_ref.shape, BF16)
    qbd_ref[0:dh, 0:tq] = qT_ref[0:dh, :]
    qbd_ref[dh:2 * dh, tq:2 * tq] = qT_ref[dh:2 * dh, :]
    l, acc = _sweep(bounded, qbd_ref[...], k_ref, vT_ref, l_ref, acc_ref, tkc)
    o = acc / l
    lv = lam_ref[...]
    lam = (jnp.exp(jnp.sum(lv[0:1] * lv[1:2], axis=-1, keepdims=True))
           - jnp.exp(jnp.sum(lv[2:3] * lv[3:4], axis=-1, keepdims=True)) + lambda_init)
    od = o[:, :tq] - lam * o[:, tq:]
    ms = jnp.mean(od * od, axis=0, keepdims=True)
    on = od * lax.rsqrt(ms + NORM_EPS) * sg_ref[...] * (1.0 - lambda_init)
    o_ref[...] = on.T.astype(o_ref.dtype)


def _diff_attn(qT, k, vT, lam_vecs, subln_g, lambda_init, bounded):
    b, w, s = qT.shape
    tq, tkc = _tile('tq', s), _tile('tkc', s)
    hw = 2 * DIFF_HEAD_DIM
    return pl.pallas_call(
        functools.partial(_diff_attn_kernel, tkc=tkc, lambda_init=lambda_init, bounded=bounded),
        grid=(b, DIFF_HEADS, s // tq),
        in_specs=[pl.BlockSpec((None, hw, tq), lambda bi, h, i: (bi, h, i)),
                  pl.BlockSpec((None, s, hw), lambda bi, h, i: (bi, 0, h)),
                  pl.BlockSpec((None, hw, s), lambda bi, h, i: (bi, h, 0)),
                  _const_spec(lam_vecs.shape), _const_spec(subln_g.shape)],
        out_specs=pl.BlockSpec((None, tq, hw), lambda bi, h, i: (bi, i, h)),
        out_shape=jax.ShapeDtypeStruct((b, s, w), BF16),
        scratch_shapes=[pltpu.VMEM((hw, 2 * tq), BF16), pltpu.VMEM((1, 2 * tq), F32),
                        pltpu.VMEM((hw, 2 * tq), F32)],
        compiler_params=_params("parallel", "parallel", "arbitrary"),
        name="diff_attn_bounded" if bounded else "diff_attn",
    )(qT, k, vT, lam_vecs, subln_g)


def _mla_attn_kernel(qT_ref, k_ref, vT_ref, o_ref, l_ref, acc_ref, *, tkc, bounded):
    l, acc = _sweep(bounded, qT_ref[...], k_ref, vT_ref, l_ref, acc_ref, tkc)
    o_ref[...] = (acc / l).T.astype(o_ref.dtype)


def _mla_attn(qT, k, vT, bounded):
    b, nh, dk, s = qT.shape
    tq, tkc = _tile('tq_mla', s), _tile('tkc', s)
    return pl.pallas_call(
        functools.partial(_mla_attn_kernel, tkc=tkc, bounded=bounded),
        grid=(b, nh, s // tq),
        in_specs=[pl.BlockSpec((None, None, dk, tq), lambda bi, h, i: (bi, h, 0, i)),
                  pl.BlockSpec((None, None, s, dk), lambda bi, h, i: (bi, h, 0, 0)),
                  pl.BlockSpec((None, MLA_V, s), lambda bi, h, i: (bi, h, 0))],
        out_specs=pl.BlockSpec((None, tq, MLA_V), lambda bi, h, i: (bi, i, h)),
        out_shape=jax.ShapeDtypeStruct((b, s, nh * MLA_V), BF16),
        scratch_shapes=[pltpu.VMEM((1, tq), F32), pltpu.VMEM((MLA_V, tq), F32)],
        compiler_params=_params("parallel", "parallel", "arbitrary"),
        name="mla_attn_bounded" if bounded else "mla_attn",
    )(qT, k, vT)


def _out2_kernel(x_ref, a_ref, b_ref, w_ref, o_ref):
    ka = a_ref.shape[1]
    o_ref[...] = x_ref[...] + _dot(a_ref[...], w_ref[:ka, :]) + _dot(b_ref[...], w_ref[ka:, :])


def _out1_kernel(x_ref, a_ref, w_ref, o_ref):
    o_ref[...] = x_ref[...] + _dot(a_ref[...], w_ref[...])


def _out_proj(x, acts, w):
    t, d = x.shape
    tm = _tile('tok', t)
    row = lambda wd: pl.BlockSpec((tm, wd), lambda i: (i, 0))
    return pl.pallas_call(
        _out2_kernel if len(acts) == 2 else _out1_kernel,
        grid=(t // tm,),
        in_specs=[row(d)] + [row(a.shape[1]) for a in acts] + [_const_spec(w.shape)],
        out_specs=row(d),
        out_shape=jax.ShapeDtypeStruct((t, d), F32),
        compiler_params=_params("parallel"),
        name="out_proj",
    )(x, *acts, w)


def _mla_in_kernel(x_ref, g_ref, wd_ref, qlg_ref, kvlg_ref, wuq_ref, wukv_ref, qgn_ref, qgr_ref, kgn_ref, kgr_ref,
                   cos_ref, sa_ref, sb_ref, qT_ref, k_ref, vT_ref):
    nh, dn, dr = MLA_HEADS, MLA_NOPE, MLA_ROPE
    ts = x_ref.shape[0]
    cos, sa, sb = cos_ref[...], sa_ref[...], sb_ref[...]
    xn = _rms(x_ref[...], g_ref[...]).astype(BF16)
    lat = _dot(xn, wd_ref[...])
    cq = _rms(lat[:, :MLA_Q_RANK], qlg_ref[...]).astype(BF16)
    ckv = _rms(lat[:, MLA_Q_RANK:MLA_Q_RANK + MLA_KV_RANK], kvlg_ref[...]).astype(BF16)
    kr = lat[:, MLA_Q_RANK + MLA_KV_RANK:]
    q = _dot(cq, wuq_ref[...])
    kv = _dot(ckv, wukv_ref[...])
    lo = lax.broadcasted_iota(jnp.int32, (ts, LANES), 1) < dr
    inv_d = 1.0 / MLA_QK
    scale = MLA_QK ** -0.5 * LOG2E

    def ssq(a):
        return jnp.sum(a * a, axis=-1, keepdims=True)

    qn_parts, qr_scale = [], []
    for pair in range(nh // 2):
        qr2 = q[:, nh * dn + LANES * pair:nh * dn + LANES * (pair + 1)]
        qr2 = qr2 * qr2
        s_even = jnp.sum(jnp.where(lo, qr2, 0.0), axis=-1, keepdims=True)
        s_odd = jnp.sum(jnp.where(lo, 0.0, qr2), axis=-1, keepdims=True)
        rs = []
        for h, s_r in ((2 * pair, s_even), (2 * pair + 1, s_odd)):
            qn_h = q[:, h * dn:(h + 1) * dn]
            r = lax.rsqrt((ssq(qn_h) + s_r) * inv_d + NORM_EPS)
            qn_parts.append(qn_h * r)
            rs.append(r)
        qr_scale.append(jnp.where(lo, rs[0], rs[1]))
    qn = jnp.concatenate(qn_parts, axis=1) * qgn_ref[...] * scale
    qr = q[:, nh * dn:] * jnp.concatenate(qr_scale, axis=1) * qgr_ref[...]
    qr = _rope_lanes(qr, cos, sa, sb) * scale
    qnT = qn.T.astype(BF16)
    qrT = qr.T.astype(BF16)
    for h in range(nh):
        qT_ref[h, 0:dn, :] = qnT[h * dn:(h + 1) * dn, :]
        qT_ref[h, dn:dn + dr, :] = qrT[h * dr:(h + 1) * dr, :]

    kr_ssq = ssq(kr)
    kr_rot = _rope_lanes(kr * kgr_ref[...], cos, sa, sb)
    kgn = kgn_ref[...]
    for h in range(nh):
        kn_h = kv[:, h * dn:(h + 1) * dn]
        r = lax.rsqrt((ssq(kn_h) + kr_ssq) * inv_d + NORM_EPS)
        k_ref[h, :, 0:dn] = (kn_h * r * kgn).astype(BF16)
        k_ref[h, :, dn:dn + dr] = (kr_rot * r)[:, :dr].astype(BF16)
    vT_ref[...] = kv[:, nh * dn:].T.astype(BF16)


def _mla_in(x, g, wd, qlg, kvlg, wuq, wukv, qgn, qgr, kgn, kgr, tables):
    b, s, d = x.shape
    ts = _tile('tok', s)
    nh = MLA_HEADS
    tab = pl.BlockSpec((ts, LANES), lambda bi, i: (i, 0))
    consts = [g, wd, qlg, kvlg, wuq, wukv, qgn, qgr, kgn, kgr]
    return pl.pallas_call(
        _mla_in_kernel,
        grid=(b, s // ts),
        in_specs=[pl.BlockSpec((None, ts, d), lambda bi, i: (bi, i, 0))]
                 + [_const_spec(c.shape) for c in consts] + [tab, tab, tab],
        out_specs=[pl.BlockSpec((None, nh, MLA_QK, ts), lambda bi, i: (bi, 0, 0, i)),
                   pl.BlockSpec((None, nh, ts, MLA_QK), lambda bi, i: (bi, 0, i, 0)),
                   pl.BlockSpec((None, nh * MLA_V, ts), lambda bi, i: (bi, 0, i))],
        out_shape=[jax.ShapeDtypeStruct((b, nh, MLA_QK, s), BF16),
                   jax.ShapeDtypeStruct((b, nh, s, MLA_QK), BF16),
                   jax.ShapeDtypeStruct((b, nh * MLA_V, s), BF16)],
        compiler_params=_params("parallel", "parallel"),
        name="mla_in",
    )(x, *consts, *tables)


def _prep_weights(ffn1_norm, ffn1_w_in, ffn1_w_out, mix_norm, ffn2_norm, ffn2_w_in, ffn2_w_out,
                  even_w_in, even_conv_w, even_q_norm, even_k_norm, even_lambda, even_subln, even_w_out,
                  mla_w_down, mla_q_lat_norm, mla_kv_lat_norm, mla_w_uq, mla_w_ukv, mla_q_norm, mla_k_norm, mla_w_o):
    nh, dn, dr = MLA_HEADS, MLA_NOPE, MLA_ROPE
    n_odd = mla_w_down.shape[0]
    row = lambda a: a[:, None, :]
    wuq = mla_w_uq.reshape(n_odd, MLA_Q_RANK, nh, MLA_QK)
    wukv = mla_w_ukv.reshape(n_odd, MLA_KV_RANK, nh, dn + MLA_V)
    tile = lambda a, k: jnp.tile(a, (1, k))[:, None, :]
    return dict(
        ffn1_norm=row(ffn1_norm), ffn1_w_in=ffn1_w_in.astype(BF16), ffn1_w_out=ffn1_w_out.astype(BF16),
        mix_norm=row(mix_norm),
        ffn2_norm=row(ffn2_norm), ffn2_w_in=ffn2_w_in.astype(BF16), ffn2_w_out=ffn2_w_out.astype(BF16),
        even_w_in=even_w_in.astype(BF16), even_conv_w=even_conv_w,
        even_qg=tile(even_q_norm, DIFF_WIDTH // DIFF_HEAD_DIM), even_kg=tile(even_k_norm, DIFF_WIDTH // DIFF_HEAD_DIM),
        even_lambda=even_lambda, even_subln=even_subln[:, :, None], even_w_out=even_w_out.astype(BF16),
        mla_w_down=jnp.pad(mla_w_down, ((0, 0), (0, 0), (0, MLA_DOWN_PAD - MLA_DOWN))).astype(BF16),
        mla_qlg=row(mla_q_lat_norm), mla_kvlg=row(mla_kv_lat_norm),
        mla_w_uq=jnp.concatenate([wuq[..., :dn].reshape(n_odd, MLA_Q_RANK, nh * dn),
                                  wuq[..., dn:].reshape(n_odd, MLA_Q_RANK, nh * dr)], axis=-1).astype(BF16),
        mla_w_ukv=jnp.concatenate([wukv[..., :dn].reshape(n_odd, MLA_KV_RANK, nh * dn),
                                   wukv[..., dn:].reshape(n_odd, MLA_KV_RANK, nh * MLA_V)], axis=-1).astype(BF16),
        mla_qgn=tile(mla_q_norm[:, :dn], nh), mla_qgr=tile(mla_q_norm[:, dn:], nh),
        mla_kgn=row(mla_k_norm[:, :dn]),
        mla_kgr=row(jnp.pad(mla_k_norm[:, dn:], ((0, 0), (0, LANES - dr)))),
        mla_w_o=mla_w_o.astype(BF16),
        even_bound=jax.vmap(functools.partial(_score_bound, DIFF_HEAD_DIM, DIFF_HEAD_DIM ** -0.5))(
            even_q_norm, even_k_norm),
        mla_bound=jax.vmap(functools.partial(_score_bound, MLA_QK, MLA_QK ** -0.5))(mla_q_norm, mla_k_norm),
    )


def _trunk(x, p):
    b, s, d = x.shape
    depth = p['ffn1_norm'].shape[0]
    tables = _rope_tables(s)
    flat = lambda a: a.reshape(b * s, a.shape[-1])
    x = flat(x)
    for l in range(depth):
        i = l // 2
        x = _ffn(x, p['ffn1_norm'][l], p['ffn1_w_in'][l], p['ffn1_w_out'][l])
        x3 = x.reshape(b, s, d)
        if l % 2 == 0:
            lambda_init = 0.8 - 0.6 * math.exp(-0.3 * l)
            ya, qT, k, vT = _even_in(x3, p['mix_norm'][l], p['even_w_in'][i], p['even_conv_w'][i],
                                     p['even_qg'][i], p['even_kg'][i], tables)
            yb = lax.cond(
                p['even_bound'][i] <= SCORE_BOUND,
                functools.partial(_diff_attn, lambda_init=lambda_init, bounded=True),
                functools.partial(_diff_attn, lambda_init=lambda_init, bounded=False),
                qT, k, vT, p['even_lambda'][i], p['even_subln'][i])
            x = _out_proj(x, [flat(ya), flat(yb)], p['even_w_out'][i])
        else:
            qT, k, vT = _mla_in(x3, p['mix_norm'][l], p['mla_w_down'][i], p['mla_qlg'][i], p['mla_kvlg'][i],
                                p['mla_w_uq'][i], p['mla_w_ukv'][i], p['mla_qgn'][i], p['mla_qgr'][i],
                                p['mla_kgn'][i], p['mla_kgr'][i], tables)
            o = lax.cond(p['mla_bound'][i] <= SCORE_BOUND,
                         functools.partial(_mla_attn, bounded=True),
                         functools.partial(_mla_attn, bounded=False), qT, k, vT)
            x = _out_proj(x, [flat(o)], p['mla_w_o'][i])
        x = _ffn(x, p['ffn2_norm'][l], p['ffn2_w_in'][l], p['ffn2_w_out'][l])
    return x.reshape(b, s, d)


def kernel(x_prompt, x_sample, ffn1_norm, ffn1_w_in, ffn1_w_out, mix_norm, ffn2_norm, ffn2_w_in, ffn2_w_out, even_w_in, even_conv_w, even_q_norm, even_k_norm, even_lambda, even_subln, even_w_out, mla_w_down, mla_q_lat_norm, mla_kv_lat_norm, mla_w_uq, mla_w_ukv, mla_q_norm, mla_k_norm, mla_w_o):
    p = _prep_weights(ffn1_norm, ffn1_w_in, ffn1_w_out, mix_norm, ffn2_norm, ffn2_w_in, ffn2_w_out,
                      even_w_in, even_conv_w, even_q_norm, even_k_norm, even_lambda, even_subln, even_w_out,
                      mla_w_down, mla_q_lat_norm, mla_kv_lat_norm, mla_w_uq, mla_w_ukv, mla_q_norm, mla_k_norm,
                      mla_w_o)
    return (_trunk(x_prompt, p), _trunk(x_sample, p))
```

```python
import functools
import math

import jax
import jax.numpy as jnp
from jax import lax
from jax.experimental import pallas as pl
from jax.experimental.pallas import tpu as pltpu

F32 = jnp.float32
BF16 = jnp.bfloat16

NORM_EPS = 1e-6
ROPE_THETA = 10000.0
LOG2E = math.log2(math.e)
SCORE_BOUND = 60.0

D_MODEL = 1024
D_FF = 2816
CONV_WIDTH = 512
DIFF_WIDTH = 512
DIFF_HEADS = 4
DIFF_HEAD_DIM = 64
EVEN_IN = 3 * CONV_WIDTH + 3 * DIFF_WIDTH
MLA_HEADS = 8
MLA_NOPE = 128
MLA_ROPE = 64
MLA_V = 128
MLA_QK = MLA_NOPE + MLA_ROPE
MLA_Q_RANK = 384
MLA_KV_RANK = 256
MLA_DOWN = MLA_Q_RANK + MLA_KV_RANK + MLA_ROPE
MLA_DOWN_PAD = 768

LANES = 128
HALO_ROWS = 8
VMEM_LIMIT = 56 * 1024 * 1024

TILES = dict(ffn=256, tok=512, tq=512, tq_mla=1024, tkc=512, unroll=8)


def _tile(name, n):
    t = min(TILES[name], n)
    assert n % t == 0, (name, n, t)
    return t


def _params(*sem):
    return pltpu.CompilerParams(dimension_semantics=sem, vmem_limit_bytes=VMEM_LIMIT)


def _const_spec(shape):
    nd = len(shape)
    return pl.BlockSpec(shape, lambda *_: (0,) * nd, pipeline_mode=pl.Buffered(1))


def _rms(x, g):
    ms = jnp.mean(x * x, axis=-1, keepdims=True)
    return x * lax.rsqrt(ms + NORM_EPS) * g


def _dot(a, b):
    return jnp.dot(a, b, preferred_element_type=F32)


def _ffn_kernel(x_ref, g_ref, win_ref, wout_ref, o_ref):
    x = x_ref[...]
    xn = _rms(x, g_ref[...]).astype(BF16)
    h = _dot(xn, win_ref[...])
    gate, up = h[:, :D_FF], h[:, D_FF:]
    act = (gate / (1.0 + jnp.exp(-gate)) * up).astype(BF16)
    o_ref[...] = x + 0.5 * _dot(act, wout_ref[...])


def _ffn(x, g, w_in, w_out):
    t, d = x.shape
    tm = _tile('ffn', t)
    return pl.pallas_call(
        _ffn_kernel,
        grid=(t // tm,),
        in_specs=[pl.BlockSpec((tm, d), lambda i: (i, 0)),
                  _const_spec((1, d)),
                  _const_spec(w_in.shape),
                  _const_spec(w_out.shape)],
        out_specs=pl.BlockSpec((tm, d), lambda i: (i, 0)),
        out_shape=jax.ShapeDtypeStruct((t, d), F32),
        compiler_params=_params("parallel"),
        name="ffn",
    )(x, g, w_in, w_out)


def _seg64_rsqrt(x):
    n = x.shape[0]
    lo = lax.broadcasted_iota(jnp.int32, (n, LANES), 1) < 64
    outs = []
    for p in range(x.shape[1] // LANES):
        x2 = x[:, LANES * p:LANES * (p + 1)]
        x2 = x2 * x2
        s_lo = jnp.sum(jnp.where(lo, x2, 0.0), axis=-1, keepdims=True)
        s_hi = jnp.sum(jnp.where(lo, 0.0, x2), axis=-1, keepdims=True)
        outs.append(lax.rsqrt(jnp.where(lo, s_lo, s_hi) * (1.0 / 64) + NORM_EPS))
    return jnp.concatenate(outs, axis=-1)


def _rope_lanes(x, cos, sa, sb):
    w = x.shape[1]
    rep = w // LANES
    if rep > 1:
        cos, sa, sb = (jnp.concatenate([t] * rep, axis=1) for t in (cos, sa, sb))
    return x * cos + pltpu.roll(x, w - 32, 1) * sa + pltpu.roll(x, 32, 1) * sb


def _rope_tables(s):
    d = 64
    inv = 1.0 / (ROPE_THETA ** (jnp.arange(0, d, 2, dtype=F32) / d))
    ang = jnp.arange(s, dtype=F32)[:, None] * inv[None, :]
    cos, sin = jnp.cos(ang), jnp.sin(ang)
    z = jnp.zeros_like(sin)
    cos = jnp.concatenate([cos, cos] * 2, axis=1)
    sa = jnp.concatenate([-sin, z] * 2, axis=1)
    sb = jnp.concatenate([z, sin] * 2, axis=1)
    return cos, sa, sb


def _even_in_kernel(x_ref, xp_ref, xn_ref, g_ref, w_ref, cw_ref, qg_ref, kg_ref, cos_ref, sa_ref, sb_ref,
                    ya_ref, qT_ref, k_ref, vT_ref):
    i = pl.program_id(1)
    n = pl.num_programs(1)
    ts = x_ref.shape[0]
    c1, c2, c3 = CONV_WIDTH, 2 * CONV_WIDTH, 3 * CONV_WIDTH
    g = g_ref[...]
    xn = _rms(x_ref[...], g).astype(BF16)
    proj = _dot(xn, w_ref[...])
    z = proj[:, c1:c2] * proj[:, c2:c3]
    xh = jnp.concatenate([xp_ref[...], xn_ref[...]], axis=0)
    ph = _dot(_rms(xh, g).astype(BF16), w_ref[:, c1:c3])
    zh = ph[:, :c1] * ph[:, c1:]
    z_before = jnp.where(i > 0, zh[HALO_ROWS - 1:HALO_ROWS], 0.0)
    z_after = jnp.where(i < n - 1, zh[HALO_ROWS:HALO_ROWS + 1], 0.0)
    row = lax.broadcasted_iota(jnp.int32, (ts, c1), 0)
    z_m1 = jnp.where(row == 0, z_before, pltpu.roll(z, 1, 0))
    z_p1 = jnp.where(row == ts - 1, z_after, pltpu.roll(z, ts - 1, 0))
    cw = cw_ref[...]
    conv = cw[0:1] * z_m1 + cw[1:2] * z + cw[2:3] * z_p1
    ya_ref[...] = (proj[:, :c1] * conv).astype(BF16)

    cos, sa, sb = cos_ref[...], sa_ref[...], sb_ref[...]
    q = proj[:, c3:c3 + DIFF_WIDTH]
    q = _rope_lanes(q * _seg64_rsqrt(q) * qg_ref[...], cos, sa, sb) * (DIFF_HEAD_DIM ** -0.5 * LOG2E)
    qT_ref[...] = q.T.astype(BF16)
    k = proj[:, c3 + DIFF_WIDTH:c3 + 2 * DIFF_WIDTH]
    k_ref[...] = _rope_lanes(k * _seg64_rsqrt(k) * kg_ref[...], cos, sa, sb).astype(BF16)
    vT_ref[...] = proj[:, c3 + 2 * DIFF_WIDTH:].T.astype(BF16)


def _even_in(x, g, w_in, conv_w, qg, kg, tables):
    b, s, d = x.shape
    ts = _tile('tok', s)
    hb = ts // HALO_ROWS
    nhb = s // HALO_ROWS
    tok = lambda w: pl.BlockSpec((None, ts, w), lambda bi, i: (bi, i, 0))
    tokT = lambda w: pl.BlockSpec((None, w, ts), lambda bi, i: (bi, 0, i))
    tab = pl.BlockSpec((ts, LANES), lambda bi, i: (i, 0))
    return pl.pallas_call(
        _even_in_kernel,
        grid=(b, s // ts),
        in_specs=[tok(d),
                  pl.BlockSpec((None, HALO_ROWS, d), lambda bi, i: (bi, jnp.maximum(i * hb - 1, 0), 0)),
                  pl.BlockSpec((None, HALO_ROWS, d), lambda bi, i: (bi, jnp.minimum((i + 1) * hb, nhb - 1), 0)),
                  _const_spec((1, d)), _const_spec(w_in.shape), _const_spec(conv_w.shape),
                  _const_spec(qg.shape), _const_spec(kg.shape), tab, tab, tab],
        out_specs=[tok(CONV_WIDTH), tokT(DIFF_WIDTH), tok(DIFF_WIDTH), tokT(DIFF_WIDTH)],
        out_shape=[jax.ShapeDtypeStruct((b, s, CONV_WIDTH), BF16),
                   jax.ShapeDtypeStruct((b, DIFF_WIDTH, s), BF16),
                   jax.ShapeDtypeStruct((b, s, DIFF_WIDTH), BF16),
                   jax.ShapeDtypeStruct((b, DIFF_WIDTH, s), BF16)],
        compiler_params=_params("parallel", "parallel"),
        name="even_in",
    )(x, x, x, g, w_in, conv_w, qg, kg, *tables)


def _flash_sweep(qmat, k_ref, vT_ref, tkc):
    nk = k_ref.shape[0] // tkc
    w = qmat.shape[1]
    dv = vT_ref.shape[0]

    def scores(c):
        off = pl.multiple_of(c * tkc, tkc)
        return _dot(k_ref[pl.ds(off, tkc), :], qmat)

    def consume(c, m, l, acc, s):
        off = pl.multiple_of(c * tkc, tkc)
        m_new = jnp.maximum(m, jnp.max(s, axis=0, keepdims=True))
        alpha = jnp.exp2(m - m_new)
        p = jnp.exp2(s - m_new)
        l = alpha * l + jnp.sum(p, axis=0, keepdims=True)
        acc = alpha * acc + _dot(vT_ref[:, pl.ds(off, tkc)], p.astype(BF16))
        return m_new, l, acc

    def step(c, carry):
        m, l, acc, s = carry
        s_next = scores(c + 1)
        return consume(c, m, l, acc, s) + (s_next,)

    init = (jnp.full((1, w), -jnp.inf, F32), jnp.zeros((1, w), F32), jnp.zeros((dv, w), F32), scores(0))
    m, l, acc, s = lax.fori_loop(0, nk - 1, step, init)
    _, l, acc = consume(nk - 1, m, l, acc, s)
    return l, acc


def _bounded_sweep(qmat, k_ref, vT_ref, l_ref, acc_ref, tkc):
    nk = k_ref.shape[0] // tkc
    l_ref[...] = jnp.zeros(l_ref.shape, F32)
    acc_ref[...] = jnp.zeros(acc_ref.shape, F32)

    def step(c, carry):
        off = pl.multiple_of(c * tkc, tkc)
        p = jnp.exp2(_dot(k_ref[pl.ds(off, tkc), :], qmat))
        l_ref[...] += jnp.sum(p, axis=0, keepdims=True)
        acc_ref[...] += _dot(vT_ref[:, pl.ds(off, tkc)], p.astype(BF16))
        return carry

    lax.fori_loop(0, nk, step, 0, unroll=min(nk, TILES['unroll']))
    return l_ref[...], acc_ref[...]


def _sweep(bounded, qmat, k_ref, vT_ref, l_ref, acc_ref, tkc):
    if bounded:
        return _bounded_sweep(qmat, k_ref, vT_ref, l_ref, acc_ref, tkc)
    return _flash_sweep(qmat, k_ref, vT_ref, tkc)


def _score_bound(d, scale, qg, kg):
    margin = 1.02
    return margin * scale * LOG2E * d * jnp.max(jnp.abs(qg)) * jnp.max(jnp.abs(kg))


def _diff_attn_kernel(qT_ref, k_ref, vT_ref, lam_ref, sg_ref, o_ref, qbd_ref, l_ref, acc_ref,
                      *, tkc, lambda_init, bounded):
    tq = qT_ref.shape[1]
    dh = DIFF_HEAD_DIM
    qbd_ref[...] = jnp.zeros(qbd_ref.shape, BF16)
    qbd_ref[0:dh, 0:tq] = qT_ref[0:dh, :]
    qbd_ref[dh:2 * dh, tq:2 * tq] = qT_ref[dh:2 * dh, :]
    l, acc = _sweep(bounded, qbd_ref[...], k_ref, vT_ref, l_ref, acc_ref, tkc)
    o = acc / l
    lv = lam_ref[...]
    lam = (jnp.exp(jnp.sum(lv[0:1] * lv[1:2], axis=-1, keepdims=True))
           - jnp.exp(jnp.sum(lv[2:3] * lv[3:4], axis=-1, keepdims=True)) + lambda_init)
    od = o[:, :tq] - lam * o[:, tq:]
    ms = jnp.mean(od * od, axis=0, keepdims=True)
    on = od * lax.rsqrt(ms + NORM_EPS) * sg_ref[...] * (1.0 - lambda_init)
    o_ref[...] = on.T.astype(o_ref.dtype)


def _diff_attn(qT, k, vT, lam_vecs, subln_g, lambda_init, bounded):
    b, w, s = qT.shape
    tq, tkc = _tile('tq', s), _tile('tkc', s)
    hw = 2 * DIFF_HEAD_DIM
    return pl.pallas_call(
        functools.partial(_diff_attn_kernel, tkc=tkc, lambda_init=lambda_init, bounded=bounded),
        grid=(b, DIFF_HEADS, s // tq),
        in_specs=[pl.BlockSpec((None, hw, tq), lambda bi, h, i: (bi, h, i)),
                  pl.BlockSpec((None, s, hw), lambda bi, h, i: (bi, 0, h)),
                  pl.BlockSpec((None, hw, s), lambda bi, h, i: (bi, h, 0)),
                  _const_spec(lam_vecs.shape), _const_spec(subln_g.shape)],
        out_specs=pl.BlockSpec((None, tq, hw), lambda bi, h, i: (bi, i, h)),
        out_shape=jax.ShapeDtypeStruct((b, s, w), BF16),
        scratch_shapes=[pltpu.VMEM((hw, 2 * tq), BF16), pltpu.VMEM((1, 2 * tq), F32),
                        pltpu.VMEM((hw, 2 * tq), F32)],
        compiler_params=_params("parallel", "parallel", "arbitrary"),
        name="diff_attn_bounded" if bounded else "diff_attn",
    )(qT, k, vT, lam_vecs, subln_g)


def _mla_attn_kernel(qT_ref, k_ref, vT_ref, o_ref, l_ref, acc_ref, *, tkc, bounded):
    l, acc = _sweep(bounded, qT_ref[...], k_ref, vT_ref, l_ref, acc_ref, tkc)
    o_ref[...] = (acc / l).T.astype(o_ref.dtype)


def _mla_attn(qT, k, vT, bounded):
    b, nh, dk, s = qT.shape
    tq, tkc = _tile('tq_mla', s), _tile('tkc', s)
    return pl.pallas_call(
        functools.partial(_mla_attn_kernel, tkc=tkc, bounded=bounded),
        grid=(b, nh, s // tq),
        in_specs=[pl.BlockSpec((None, None, dk, tq), lambda bi, h, i: (bi, h, 0, i)),
                  pl.BlockSpec((None, None, s, dk), lambda bi, h, i: (bi, h, 0, 0)),
                  pl.BlockSpec((None, MLA_V, s), lambda bi, h, i: (bi, h, 0))],
        out_specs=pl.BlockSpec((None, tq, MLA_V), lambda bi, h, i: (bi, i, h)),
        out_shape=jax.ShapeDtypeStruct((b, s, nh * MLA_V), BF16),
        scratch_shapes=[pltpu.VMEM((1, tq), F32), pltpu.VMEM((MLA_V, tq), F32)],
        compiler_params=_params("parallel", "parallel", "arbitrary"),
        name="mla_attn_bounded" if bounded else "mla_attn",
    )(qT, k, vT)


def _out2_kernel(x_ref, a_ref, b_ref, w_ref, o_ref):
    ka = a_ref.shape[1]
    o_ref[...] = x_ref[...] + _dot(a_ref[...], w_ref[:ka, :]) + _dot(b_ref[...], w_ref[ka:, :])


def _out1_kernel(x_ref, a_ref, w_ref, o_ref):
    o_ref[...] = x_ref[...] + _dot(a_ref[...], w_ref[...])


def _out_proj(x, acts, w):
    t, d = x.shape
    tm = _tile('tok', t)
    row = lambda wd: pl.BlockSpec((tm, wd), lambda i: (i, 0))
    return pl.pallas_call(
        _out2_kernel if len(acts) == 2 else _out1_kernel,
        grid=(t // tm,),
        in_specs=[row(d)] + [row(a.shape[1]) for a in acts] + [_const_spec(w.shape)],
        out_specs=row(d),
        out_shape=jax.ShapeDtypeStruct((t, d), F32),
        compiler_params=_params("parallel"),
        name="out_proj",
    )(x, *acts, w)


def _mla_in_kernel(x_ref, g_ref, wd_ref, qlg_ref, kvlg_ref, wuq_ref, wukv_ref, qgn_ref, qgr_ref, kgn_ref, kgr_ref,
                   cos_ref, sa_ref, sb_ref, qT_ref, k_ref, vT_ref):
    nh, dn, dr = MLA_HEADS, MLA_NOPE, MLA_ROPE
    ts = x_ref.shape[0]
    cos, sa, sb = cos_ref[...], sa_ref[...], sb_ref[...]
    xn = _rms(x_ref[...], g_ref[...]).astype(BF16)
    lat = _dot(xn, wd_ref[...])
    cq = _rms(lat[:, :MLA_Q_RANK], qlg_ref[...]).astype(BF16)
    ckv = _rms(lat[:, MLA_Q_RANK:MLA_Q_RANK + MLA_KV_RANK], kvlg_ref[...]).astype(BF16)
    kr = lat[:, MLA_Q_RANK + MLA_KV_RANK:]
    q = _dot(cq, wuq_ref[...])
    kv = _dot(ckv, wukv_ref[...])
    lo = lax.broadcasted_iota(jnp.int32, (ts, LANES), 1) < dr
    inv_d = 1.0 / MLA_QK
    scale = MLA_QK ** -0.5 * LOG2E

    def ssq(a):
        return jnp.sum(a * a, axis=-1, keepdims=True)

    qn_parts, qr_scale = [], []
    for pair in range(nh // 2):
        qr2 = q[:, nh * dn + LANES * pair:nh * dn + LANES * (pair + 1)]
        qr2 = qr2 * qr2
        s_even = jnp.sum(jnp.where(lo, qr2, 0.0), axis=-1, keepdims=True)
        s_odd = jnp.sum(jnp.where(lo, 0.0, qr2), axis=-1, keepdims=True)
        rs = []
        for h, s_r in ((2 * pair, s_even), (2 * pair + 1, s_odd)):
            qn_h = q[:, h * dn:(h + 1) * dn]
            r = lax.rsqrt((ssq(qn_h) + s_r) * inv_d + NORM_EPS)
            qn_parts.append(qn_h * r)
            rs.append(r)
        qr_scale.append(jnp.where(lo, rs[0], rs[1]))
    qn = jnp.concatenate(qn_parts, axis=1) * qgn_ref[...] * scale
    qr = q[:, nh * dn:] * jnp.concatenate(qr_scale, axis=1) * qgr_ref[...]
    qr = _rope_lanes(qr, cos, sa, sb) * scale
    qnT = qn.T.astype(BF16)
    qrT = qr.T.astype(BF16)
    for h in range(nh):
        qT_ref[h, 0:dn, :] = qnT[h * dn:(h + 1) * dn, :]
        qT_ref[h, dn:dn + dr, :] = qrT[h * dr:(h + 1) * dr, :]

    kr_ssq = ssq(kr)
    kr_rot = _rope_lanes(kr * kgr_ref[...], cos, sa, sb)
    kgn = kgn_ref[...]
    for h in range(nh):
        kn_h = kv[:, h * dn:(h + 1) * dn]
        r = lax.rsqrt((ssq(kn_h) + kr_ssq) * inv_d + NORM_EPS)
        k_ref[h, :, 0:dn] = (kn_h * r * kgn).astype(BF16)
        k_ref[h, :, dn:dn + dr] = (kr_rot * r)[:, :dr].astype(BF16)
    vT_ref[...] = kv[:, nh * dn:].T.astype(BF16)


def _mla_in(x, g, wd, qlg, kvlg, wuq, wukv, qgn, qgr, kgn, kgr, tables):
    b, s, d = x.shape
    ts = _tile('tok', s)
    nh = MLA_HEADS
    tab = pl.BlockSpec((ts, LANES), lambda bi, i: (i, 0))
    consts = [g, wd, qlg, kvlg, wuq, wukv, qgn, qgr, kgn, kgr]
    return pl.pallas_call(
        _mla_in_kernel,
        grid=(b, s // ts),
        in_specs=[pl.BlockSpec((None, ts, d), lambda bi, i: (bi, i, 0))]
                 + [_const_spec(c.shape) for c in consts] + [tab, tab, tab],
        out_specs=[pl.BlockSpec((None, nh, MLA_QK, ts), lambda bi, i: (bi, 0, 0, i)),
                   pl.BlockSpec((None, nh, ts, MLA_QK), lambda bi, i: (bi, 0, i, 0)),
                   pl.BlockSpec((None, nh * MLA_V, ts), lambda bi, i: (bi, 0, i))],
        out_shape=[jax.ShapeDtypeStruct((b, nh, MLA_QK, s), BF16),
                   jax.ShapeDtypeStruct((b, nh, s, MLA_QK), BF16),
                   jax.ShapeDtypeStruct((b, nh * MLA_V, s), BF16)],
        compiler_params=_params("parallel", "parallel"),
        name="mla_in",
    )(x, *consts, *tables)


def _prep_weights(ffn1_norm, ffn1_w_in, ffn1_w_out, mix_norm, ffn2_norm, ffn2_w_in, ffn2_w_out,
                  even_w_in, even_conv_w, even_q_norm, even_k_norm, even_lambda, even_subln, even_w_out,
                  mla_w_down, mla_q_lat_norm, mla_kv_lat_norm, mla_w_uq, mla_w_ukv, mla_q_norm, mla_k_norm, mla_w_o):
    nh, dn, dr = MLA_HEADS, MLA_NOPE, MLA_ROPE
    n_odd = mla_w_down.shape[0]
    row = lambda a: a[:, None, :]
    wuq = mla_w_uq.reshape(n_odd, MLA_Q_RANK, nh, MLA_QK)
    wukv = mla_w_ukv.reshape(n_odd, MLA_KV_RANK, nh, dn + MLA_V)
    tile = lambda a, k: jnp.tile(a, (1, k))[:, None, :]
    return dict(
        ffn1_norm=row(ffn1_norm), ffn1_w_in=ffn1_w_in.astype(BF16), ffn1_w_out=ffn1_w_out.astype(BF16),
        mix_norm=row(mix_norm),
        ffn2_norm=row(ffn2_norm), ffn2_w_in=ffn2_w_in.astype(BF16), ffn2_w_out=ffn2_w_out.astype(BF16),
        even_w_in=even_w_in.astype(BF16), even_conv_w=even_conv_w,
        even_qg=tile(even_q_norm, DIFF_WIDTH // DIFF_HEAD_DIM), even_kg=tile(even_k_norm, DIFF_WIDTH // DIFF_HEAD_DIM),
        even_lambda=even_lambda, even_subln=even_subln[:, :, None], even_w_out=even_w_out.astype(BF16),
        mla_w_down=jnp.pad(mla_w_down, ((0, 0), (0, 0), (0, MLA_DOWN_PAD - MLA_DOWN))).astype(BF16),
        mla_qlg=row(mla_q_lat_norm), mla_kvlg=row(mla_kv_lat_norm),
        mla_w_uq=jnp.concatenate([wuq[..., :dn].reshape(n_odd, MLA_Q_RANK, nh * dn),
                                  wuq[..., dn:].reshape(n_odd, MLA_Q_RANK, nh * dr)], axis=-1).astype(BF16),
        mla_w_ukv=jnp.concatenate([wukv[..., :dn].reshape(n_odd, MLA_KV_RANK, nh * dn),
                                   wukv[..., dn:].reshape(n_odd, MLA_KV_RANK, nh * MLA_V)], axis=-1).astype(BF16),
        mla_qgn=tile(mla_q_norm[:, :dn], nh), mla_qgr=tile(mla_q_norm[:, dn:], nh),
        mla_kgn=row(mla_k_norm[:, :dn]),
        mla_kgr=row(jnp.pad(mla_k_norm[:, dn:], ((0, 0), (0, LANES - dr)))),
        mla_w_o=mla_w_o.astype(BF16),
        even_bound=jax.vmap(functools.partial(_score_bound, DIFF_HEAD_DIM, DIFF_HEAD_DIM ** -0.5))(
            even_q_norm, even_k_norm),
        mla_bound=jax.vmap(functools.partial(_score_bound, MLA_QK, MLA_QK ** -0.5))(mla_q_norm, mla_k_norm),
    )


def _trunk(x, p):
    b, s, d = x.shape
    depth = p['ffn1_norm'].shape[0]
    tables = _rope_tables(s)
    flat = lambda a: a.reshape(b * s, a.shape[-1])
    x = flat(x)
    for l in range(depth):
        i = l // 2
        x = _ffn(x, p['ffn1_norm'][l], p['ffn1_w_in'][l], p['ffn1_w_out'][l])
        x3 = x.reshape(b, s, d)
        if l % 2 == 0:
            lambda_init = 0.8 - 0.6 * math.exp(-0.3 * l)
            ya, qT, k, vT = _even_in(x3, p['mix_norm'][l], p['even_w_in'][i], p['even_conv_w'][i],
                                     p['even_qg'][i], p['even_kg'][i], tables)
            yb = lax.cond(
                p['even_bound'][i] <= SCORE_BOUND,
                functools.partial(_diff_attn, lambda_init=lambda_init, bounded=True),
                functools.partial(_diff_attn, lambda_init=lambda_init, bounded=False),
                qT, k, vT, p['even_lambda'][i], p['even_subln'][i])
            x = _out_proj(x, [flat(ya), flat(yb)], p['even_w_out'][i])
        else:
            qT, k, vT = _mla_in(x3, p['mix_norm'][l], p['mla_w_down'][i], p['mla_qlg'][i], p['mla_kvlg'][i],
                                p['mla_w_uq'][i], p['mla_w_ukv'][i], p['mla_qgn'][i], p['mla_qgr'][i],
                                p['mla_kgn'][i], p['mla_kgr'][i], tables)
            o = lax.cond(p['mla_bound'][i] <= SCORE_BOUND,
                         functools.partial(_mla_attn, bounded=True),
                         functools.partial(_mla_attn, bounded=False), qT, k, vT)
            x = _out_proj(x, [flat(o)], p['mla_w_o'][i])
        x = _ffn(x, p['ffn2_norm'][l], p['ffn2_w_in'][l], p['ffn2_w_out'][l])
    return x.reshape(b, s, d)


def kernel(x_prompt, x_sample, ffn1_norm, ffn1_w_in, ffn1_w_out, mix_norm, ffn2_norm, ffn2_w_in, ffn2_w_out, even_w_in, even_conv_w, even_q_norm, even_k_norm, even_lambda, even_subln, even_w_out, mla_w_down, mla_q_lat_norm, mla_kv_lat_norm, mla_w_uq, mla_w_ukv, mla_q_norm, mla_k_norm, mla_w_o):
    p = _prep_weights(ffn1_norm, ffn1_w_in, ffn1_w_out, mix_norm, ffn2_norm, ffn2_w_in, ffn2_w_out,
                      even_w_in, even_conv_w, even_q_norm, even_k_norm, even_lambda, even_subln, even_w_out,
                      mla_w_down, mla_q_lat_norm, mla_kv_lat_norm, mla_w_uq, mla_w_ukv, mla_q_norm, mla_k_norm,
                      mla_w_o)
    return (_trunk(x_prompt, p), _trunk(x_sample, p))
```

```python
import functools
import math

import jax
import jax.numpy as jnp
from jax import lax
from jax.experimental import pallas as pl
from jax.experimental.pallas import tpu as pltpu

F32 = jnp.float32
BF16 = jnp.bfloat16

NORM_EPS = 1e-6
ROPE_THETA = 10000.0
LOG2E = math.log2(math.e)
SCORE_BOUND = 60.0

D_MODEL = 1024
D_FF = 2816
CONV_WIDTH = 512
DIFF_WIDTH = 512
DIFF_HEADS = 4
DIFF_HEAD_DIM = 64
EVEN_IN = 3 * CONV_WIDTH + 3 * DIFF_WIDTH
MLA_HEADS = 8
MLA_NOPE = 128
MLA_ROPE = 64
MLA_V = 128
MLA_QK = MLA_NOPE + MLA_ROPE
MLA_Q_RANK = 384
MLA_KV_RANK = 256
MLA_DOWN = MLA_Q_RANK + MLA_KV_RANK + MLA_ROPE
MLA_DOWN_PAD = 768

LANES = 128
HALO_ROWS = 8
VMEM_LIMIT = 56 * 1024 * 1024

TILES = dict(ffn=512, tok=512, tq=1024, tq_mla=1024, tkc=512, unroll=8)


def _tile(name, n):
    t = min(TILES[name], n)
    assert n % t == 0, (name, n, t)
    return t


def _params(*sem):
    return pltpu.CompilerParams(dimension_semantics=sem, vmem_limit_bytes=VMEM_LIMIT)


def _const_spec(shape):
    nd = len(shape)
    return pl.BlockSpec(shape, lambda *_: (0,) * nd, pipeline_mode=pl.Buffered(1))


def _rms(x, g):
    ms = jnp.mean(x * x, axis=-1, keepdims=True)
    return x * lax.rsqrt(ms + NORM_EPS) * g


def _dot(a, b):
    return jnp.dot(a, b, preferred_element_type=F32)


def _ffn_kernel(x_ref, g_ref, win_ref, wout_ref, o_ref):
    x = x_ref[...]
    xn = _rms(x, g_ref[...]).astype(BF16)
    h = _dot(xn, win_ref[...])
    gate, up = h[:, :D_FF], h[:, D_FF:]
    act = (gate / (1.0 + jnp.exp(-gate)) * up).astype(BF16)
    o_ref[...] = x + 0.5 * _dot(act, wout_ref[...])


def _ffn(x, g, w_in, w_out):
    t, d = x.shape
    tm = _tile('ffn', t)
    return pl.pallas_call(
        _ffn_kernel,
        grid=(t // tm,),
        in_specs=[pl.BlockSpec((tm, d), lambda i: (i, 0)),
                  _const_spec((1, d)),
                  _const_spec(w_in.shape),
                  _const_spec(w_out.shape)],
        out_specs=pl.BlockSpec((tm, d), lambda i: (i, 0)),
        out_shape=jax.ShapeDtypeStruct((t, d), F32),
        compiler_params=_params("parallel"),
        name="ffn",
    )(x, g, w_in, w_out)


def _seg64_rsqrt(x):
    n = x.shape[0]
    lo = lax.broadcasted_iota(jnp.int32, (n, LANES), 1) < 64
    outs = []
    for p in range(x.shape[1] // LANES):
        x2 = x[:, LANES * p:LANES * (p + 1)]
        x2 = x2 * x2
        s_lo = jnp.sum(jnp.where(lo, x2, 0.0), axis=-1, keepdims=True)
        s_hi = jnp.sum(jnp.where(lo, 0.0, x2), axis=-1, keepdims=True)
        outs.append(lax.rsqrt(jnp.where(lo, s_lo, s_hi) * (1.0 / 64) + NORM_EPS))
    return jnp.concatenate(outs, axis=-1)


def _rope_lanes(x, cos, sa, sb):
    w = x.shape[1]
    rep = w // LANES
    if rep > 1:
        cos, sa, sb = (jnp.concatenate([t] * rep, axis=1) for t in (cos, sa, sb))
    return x * cos + pltpu.roll(x, w - 32, 1) * sa + pltpu.roll(x, 32, 1) * sb


def _rope_tables(s):
    d = 64
    inv = 1.0 / (ROPE_THETA ** (jnp.arange(0, d, 2, dtype=F32) / d))
    ang = jnp.arange(s, dtype=F32)[:, None] * inv[None, :]
    cos, sin = jnp.cos(ang), jnp.sin(ang)
    z = jnp.zeros_like(sin)
    cos = jnp.concatenate([cos, cos] * 2, axis=1)
    sa = jnp.concatenate([-sin, z] * 2, axis=1)
    sb = jnp.concatenate([z, sin] * 2, axis=1)
    return cos, sa, sb


def _even_in_kernel(x_ref, xp_ref, xn_ref, g_ref, w_ref, cw_ref, qg_ref, kg_ref, cos_ref, sa_ref, sb_ref,
                    ya_ref, qT_ref, k_ref, vT_ref):
    i = pl.program_id(1)
    n = pl.num_programs(1)
    ts = x_ref.shape[0]
    c1, c2, c3 = CONV_WIDTH, 2 * CONV_WIDTH, 3 * CONV_WIDTH
    g = g_ref[...]
    xn = _rms(x_ref[...], g).astype(BF16)
    proj = _dot(xn, w_ref[...])
    z = proj[:, c1:c2] * proj[:, c2:c3]
    xh = jnp.concatenate([xp_ref[...], xn_ref[...]], axis=0)
    ph = _dot(_rms(xh, g).astype(BF16), w_ref[:, c1:c3])
    zh = ph[:, :c1] * ph[:, c1:]
    z_before = jnp.where(i > 0, zh[HALO_ROWS - 1:HALO_ROWS], 0.0)
    z_after = jnp.where(i < n - 1, zh[HALO_ROWS:HALO_ROWS + 1], 0.0)
    row = lax.broadcasted_iota(jnp.int32, (ts, c1), 0)
    z_m1 = jnp.where(row == 0, z_before, pltpu.roll(z, 1, 0))
    z_p1 = jnp.where(row == ts - 1, z_after, pltpu.roll(z, ts - 1, 0))
    cw = cw_ref[...]
    conv = cw[0:1] * z_m1 + cw[1:2] * z + cw[2:3] * z_p1
    ya_ref[...] = (proj[:, :c1] * conv).astype(BF16)

    cos, sa, sb = cos_ref[...], sa_ref[...], sb_ref[...]
    q = proj[:, c3:c3 + DIFF_WIDTH]
    q = _rope_lanes(q * _seg64_rsqrt(q) * qg_ref[...], cos, sa, sb) * (DIFF_HEAD_DIM ** -0.5 * LOG2E)
    qT_ref[...] = q.T.astype(BF16)
    k = proj[:, c3 + DIFF_WIDTH:c3 + 2 * DIFF_WIDTH]
    k_ref[...] = _rope_lanes(k * _seg64_rsqrt(k) * kg_ref[...], cos, sa, sb).astype(BF16)
    vT_ref[...] = proj[:, c3 + 2 * DIFF_WIDTH:].T.astype(BF16)


def _even_in(x, g, w_in, conv_w, qg, kg, tables):
    b, s, d = x.shape
    ts = _tile('tok', s)
    hb = ts // HALO_ROWS
    nhb = s // HALO_ROWS
    tok = lambda w: pl.BlockSpec((None, ts, w), lambda bi, i: (bi, i, 0))
    tokT = lambda w: pl.BlockSpec((None, w, ts), lambda bi, i: (bi, 0, i))
    tab = pl.BlockSpec((ts, LANES), lambda bi, i: (i, 0))
    return pl.pallas_call(
        _even_in_kernel,
        grid=(b, s // ts),
        in_specs=[tok(d),
                  pl.BlockSpec((None, HALO_ROWS, d), lambda bi, i: (bi, jnp.maximum(i * hb - 1, 0), 0)),
                  pl.BlockSpec((None, HALO_ROWS, d), lambda bi, i: (bi, jnp.minimum((i + 1) * hb, nhb - 1), 0)),
                  _const_spec((1, d)), _const_spec(w_in.shape), _const_spec(conv_w.shape),
                  _const_spec(qg.shape), _const_spec(kg.shape), tab, tab, tab],
        out_specs=[tok(CONV_WIDTH), tokT(DIFF_WIDTH), tok(DIFF_WIDTH), tokT(DIFF_WIDTH)],
        out_shape=[jax.ShapeDtypeStruct((b, s, CONV_WIDTH), BF16),
                   jax.ShapeDtypeStruct((b, DIFF_WIDTH, s), BF16),
                   jax.ShapeDtypeStruct((b, s, DIFF_WIDTH), BF16),
                   jax.ShapeDtypeStruct((b, DIFF_WIDTH, s), BF16)],
        compiler_params=_params("parallel", "parallel"),
        name="even_in",
    )(x, x, x, g, w_in, conv_w, qg, kg, *tables)


def _flash_sweep(qmat, k_ref, vT_ref, tkc):
    nk = k_ref.shape[0] // tkc
    w = qmat.shape[1]
    dv = vT_ref.shape[0]

    def scores(c):
        off = pl.multiple_of(c * tkc, tkc)
        return _dot(k_ref[pl.ds(off, tkc), :], qmat)

    def consume(c, m, l, acc, s):
        off = pl.multiple_of(c * tkc, tkc)
        m_new = jnp.maximum(m, jnp.max(s, axis=0, keepdims=True))
        alpha = jnp.exp2(m - m_new)
        p = jnp.exp2(s - m_new)
        l = alpha * l + jnp.sum(p, axis=0, keepdims=True)
        acc = alpha * acc + _dot(vT_ref[:, pl.ds(off, tkc)], p.astype(BF16))
        return m_new, l, acc

    def step(c, carry):
        m, l, acc, s = carry
        s_next = scores(c + 1)
        return consume(c, m, l, acc, s) + (s_next,)

    init = (jnp.full((1, w), -jnp.inf, F32), jnp.zeros((1, w), F32), jnp.zeros((dv, w), F32), scores(0))
    m, l, acc, s = lax.fori_loop(0, nk - 1, step, init)
    _, l, acc = consume(nk - 1, m, l, acc, s)
    return l, acc


def _bounded_sweep(qmat, k_ref, vT_ref, l_ref, acc_ref, tkc):
    nk = k_ref.shape[0] // tkc
    l_ref[...] = jnp.zeros(l_ref.shape, F32)
    acc_ref[...] = jnp.zeros(acc_ref.shape, F32)

    def step(c, carry):
        off = pl.multiple_of(c * tkc, tkc)
        p = jnp.exp2(_dot(k_ref[pl.ds(off, tkc), :], qmat))
        l_ref[...] += jnp.sum(p, axis=0, keepdims=True)
        acc_ref[...] += _dot(vT_ref[:, pl.ds(off, tkc)], p.astype(BF16))
        return carry

    lax.fori_loop(0, nk, step, 0, unroll=min(nk, TILES['unroll']))
    return l_ref[...], acc_ref[...]


def _sweep(bounded, qmat, k_ref, vT_ref, l_ref, acc_ref, tkc):
    if bounded:
        return _bounded_sweep(qmat, k_ref, vT_ref, l_ref, acc_ref, tkc)
    return _flash_sweep(qmat, k_ref, vT_ref, tkc)


def _score_bound(d, scale, qg, kg):
    margin = 1.02
    return margin * scale * LOG2E * d * jnp.max(jnp.abs(qg)) * jnp.max(jnp.abs(kg))


def _diff_attn_kernel(qT_ref, k_ref, vT_ref, lam_ref, sg_ref, o_ref, qbd_ref, l_ref, acc_ref,
                      *, tkc, lambda_init, bounded):
    tq = qT_ref.shape[1]
    dh = DIFF_HEAD_DIM
    qbd_ref[...] = jnp.zeros(qbd_ref.shape, BF16)
    qbd_ref[0:dh, 0:tq] = qT_ref[0:dh, :]
    qbd_ref[dh:2 * dh, tq:2 * tq] = qT_ref[dh:2 * dh, :]
    l, acc = _sweep(bounded, qbd_ref[...], k_ref, vT_ref, l_ref, acc_ref, tkc)
    o = acc / l
    lv = lam_ref[...]
    lam = (jnp.exp(jnp.sum(lv[0:1] * lv[1:2], axis=-1, keepdims=True))
           - jnp.exp(jnp.sum(lv[2:3] * lv[3:4], axis=-1, keepdims=True)) + lambda_init)
    od = o[:, :tq] - lam * o[:, tq:]
    ms = jnp.mean(od * od, axis=0, keepdims=True)
    on = od * lax.rsqrt(ms + NORM_EPS) * sg_ref[...] * (1.0 - lambda_init)
    o_ref[...] = on.T.astype(o_ref.dtype)


def _diff_attn(qT, k, vT, lam_vecs, subln_g, lambda_init, bounded):
    b, w, s = qT.shape
    tq, tkc = _tile('tq', s), _tile('tkc', s)
    hw = 2 * DIFF_HEAD_DIM
    return pl.pallas_call(
        functools.partial(_diff_attn_kernel, tkc=tkc, lambda_init=lambda_init, bounded=bounded),
        grid=(b, DIFF_HEADS, s // tq),
        in_specs=[pl.BlockSpec((None, hw, tq), lambda bi, h, i: (bi, h, i)),
                  pl.BlockSpec((None, s, hw), lambda bi, h, i: (bi, 0, h)),
                  pl.BlockSpec((None, hw, s), lambda bi, h, i: (bi, h, 0)),
                  _const_spec(lam_vecs.shape), _const_spec(subln_g.shape)],
        out_specs=pl.BlockSpec((None, tq, hw), lambda bi, h, i: (bi, i, h)),
        out_shape=jax.ShapeDtypeStruct((b, s, w), BF16),
        scratch_shapes=[pltpu.VMEM((hw, 2 * tq), BF16), pltpu.VMEM((1, 2 * tq), F32),
                        pltpu.VMEM((hw, 2 * tq), F32)],
        compiler_params=_params("parallel", "parallel", "arbitrary"),
        name="diff_attn_bounded" if bounded else "diff_attn",
    )(qT, k, vT, lam_vecs, subln_g)


def _mla_attn_kernel(qT_ref, k_ref, vT_ref, o_ref, l_ref, acc_ref, *, tkc, bounded):
    l, acc = _sweep(bounded, qT_ref[...], k_ref, vT_ref, l_ref, acc_ref, tkc)
    o_ref[...] = (acc / l).T.astype(o_ref.dtype)


def _mla_attn(qT, k, vT, bounded):
    b, nh, dk, s = qT.shape
    tq, tkc = _tile('tq_mla', s), _tile('tkc', s)
    return pl.pallas_call(
        functools.partial(_mla_attn_kernel, tkc=tkc, bounded=bounded),
        grid=(b, nh, s // tq),
        in_specs=[pl.BlockSpec((None, None, dk, tq), lambda bi, h, i: (bi, h, 0, i)),
                  pl.BlockSpec((None, None, s, dk), lambda bi, h, i: (bi, h, 0, 0)),
                  pl.BlockSpec((None, MLA_V, s), lambda bi, h, i: (bi, h, 0))],
        out_specs=pl.BlockSpec((None, tq, MLA_V), lambda bi, h, i: (bi, i, h)),
        out_shape=jax.ShapeDtypeStruct((b, s, nh * MLA_V), BF16),
        scratch_shapes=[pltpu.VMEM((1, tq), F32), pltpu.VMEM((MLA_V, tq), F32)],
        compiler_params=_params("parallel", "parallel", "arbitrary"),
        name="mla_attn_bounded" if bounded else "mla_attn",
    )(qT, k, vT)


def _out2_kernel(x_ref, a_ref, b_ref, w_ref, o_ref):
    ka = a_ref.shape[1]
    o_ref[...] = x_ref[...] + _dot(a_ref[...], w_ref[:ka, :]) + _dot(b_ref[...], w_ref[ka:, :])


def _out1_kernel(x_ref, a_ref, w_ref, o_ref):
    o_ref[...] = x_ref[...] + _dot(a_ref[...], w_ref[...])


def _out_proj(x, acts, w):
    t, d = x.shape
    tm = _tile('tok', t)
    row = lambda wd: pl.BlockSpec((tm, wd), lambda i: (i, 0))
    return pl.pallas_call(
        _out2_kernel if len(acts) == 2 else _out1_kernel,
        grid=(t // tm,),
        in_specs=[row(d)] + [row(a.shape[1]) for a in acts] + [_const_spec(w.shape)],
        out_specs=row(d),
        out_shape=jax.ShapeDtypeStruct((t, d), F32),
        compiler_params=_params("parallel"),
        name="out_proj",
    )(x, *acts, w)


def _mla_in_kernel(x_ref, g_ref, wd_ref, qlg_ref, kvlg_ref, wuq_ref, wukv_ref, qgn_ref, qgr_ref, kgn_ref, kgr_ref,
                   cos_ref, sa_ref, sb_ref, qT_ref, k_ref, vT_ref):
    nh, dn, dr = MLA_HEADS, MLA_NOPE, MLA_ROPE
    ts = x_ref.shape[0]
    cos, sa, sb = cos_ref[...], sa_ref[...], sb_ref[...]
    xn = _rms(x_ref[...], g_ref[...]).astype(BF16)
    lat = _dot(xn, wd_ref[...])
    cq = _rms(lat[:, :MLA_Q_RANK], qlg_ref[...]).astype(BF16)
    ckv = _rms(lat[:, MLA_Q_RANK:MLA_Q_RANK + MLA_KV_RANK], kvlg_ref[...]).astype(BF16)
    kr = lat[:, MLA_Q_RANK + MLA_KV_RANK:]
    q = _dot(cq, wuq_ref[...])
    kv = _dot(ckv, wukv_ref[...])
    lo = lax.broadcasted_iota(jnp.int32, (ts, LANES), 1) < dr
    inv_d = 1.0 / MLA_QK
    scale = MLA_QK ** -0.5 * LOG2E

    def ssq(a):
        return jnp.sum(a * a, axis=-1, keepdims=True)

    qn_parts, qr_scale = [], []
    for pair in range(nh // 2):
        qr2 = q[:, nh * dn + LANES * pair:nh * dn + LANES * (pair + 1)]
        qr2 = qr2 * qr2
        s_even = jnp.sum(jnp.where(lo, qr2, 0.0), axis=-1, keepdims=True)
        s_odd = jnp.sum(jnp.where(lo, 0.0, qr2), axis=-1, keepdims=True)
        rs = []
        for h, s_r in ((2 * pair, s_even), (2 * pair + 1, s_odd)):
            qn_h = q[:, h * dn:(h + 1) * dn]
            r = lax.rsqrt((ssq(qn_h) + s_r) * inv_d + NORM_EPS)
            qn_parts.append(qn_h * r)
            rs.append(r)
        qr_scale.append(jnp.where(lo, rs[0], rs[1]))
    qn = jnp.concatenate(qn_parts, axis=1) * qgn_ref[...] * scale
    qr = q[:, nh * dn:] * jnp.concatenate(qr_scale, axis=1) * qgr_ref[...]
    qr = _rope_lanes(qr, cos, sa, sb) * scale
    qnT = qn.T.astype(BF16)
    qrT = qr.T.astype(BF16)
    for h in range(nh):
        qT_ref[h, 0:dn, :] = qnT[h * dn:(h + 1) * dn, :]
        qT_ref[h, dn:dn + dr, :] = qrT[h * dr:(h + 1) * dr, :]

    kr_ssq = ssq(kr)
    kr_rot = _rope_lanes(kr * kgr_ref[...], cos, sa, sb)
    kgn = kgn_ref[...]
    for h in range(nh):
        kn_h = kv[:, h * dn:(h + 1) * dn]
        r = lax.rsqrt((ssq(kn_h) + kr_ssq) * inv_d + NORM_EPS)
        k_ref[h, :, 0:dn] = (kn_h * r * kgn).astype(BF16)
        k_ref[h, :, dn:dn + dr] = (kr_rot * r)[:, :dr].astype(BF16)
    vT_ref[...] = kv[:, nh * dn:].T.astype(BF16)


def _mla_in(x, g, wd, qlg, kvlg, wuq, wukv, qgn, qgr, kgn, kgr, tables):
    b, s, d = x.shape
    ts = _tile('tok', s)
    nh = MLA_HEADS
    tab = pl.BlockSpec((ts, LANES), lambda bi, i: (i, 0))
    consts = [g, wd, qlg, kvlg, wuq, wukv, qgn, qgr, kgn, kgr]
    return pl.pallas_call(
        _mla_in_kernel,
        grid=(b, s // ts),
        in_specs=[pl.BlockSpec((None, ts, d), lambda bi, i: (bi, i, 0))]
                 + [_const_spec(c.shape) for c in consts] + [tab, tab, tab],
        out_specs=[pl.BlockSpec((None, nh, MLA_QK, ts), lambda bi, i: (bi, 0, 0, i)),
                   pl.BlockSpec((None, nh, ts, MLA_QK), lambda bi, i: (bi, 0, i, 0)),
                   pl.BlockSpec((None, nh * MLA_V, ts), lambda bi, i: (bi, 0, i))],
        out_shape=[jax.ShapeDtypeStruct((b, nh, MLA_QK, s), BF16),
                   jax.ShapeDtypeStruct((b, nh, s, MLA_QK), BF16),
                   jax.ShapeDtypeStruct((b, nh * MLA_V, s), BF16)],
        compiler_params=_params("parallel", "parallel"),
        name="mla_in",
    )(x, *consts, *tables)


def _prep_weights(ffn1_norm, ffn1_w_in, ffn1_w_out, mix_norm, ffn2_norm, ffn2_w_in, ffn2_w_out,
                  even_w_in, even_conv_w, even_q_norm, even_k_norm, even_lambda, even_subln, even_w_out,
                  mla_w_down, mla_q_lat_norm, mla_kv_lat_norm, mla_w_uq, mla_w_ukv, mla_q_norm, mla_k_norm, mla_w_o):
    nh, dn, dr = MLA_HEADS, MLA_NOPE, MLA_ROPE
    n_odd = mla_w_down.shape[0]
    row = lambda a: a[:, None, :]
    wuq = mla_w_uq.reshape(n_odd, MLA_Q_RANK, nh, MLA_QK)
    wukv = mla_w_ukv.reshape(n_odd, MLA_KV_RANK, nh, dn + MLA_V)
    tile = lambda a, k: jnp.tile(a, (1, k))[:, None, :]
    return dict(
        ffn1_norm=row(ffn1_norm), ffn1_w_in=ffn1_w_in.astype(BF16), ffn1_w_out=ffn1_w_out.astype(BF16),
        mix_norm=row(mix_norm),
        ffn2_norm=row(ffn2_norm), ffn2_w_in=ffn2_w_in.astype(BF16), ffn2_w_out=ffn2_w_out.astype(BF16),
        even_w_in=even_w_in.astype(BF16), even_conv_w=even_conv_w,
        even_qg=tile(even_q_norm, DIFF_WIDTH // DIFF_HEAD_DIM), even_kg=tile(even_k_norm, DIFF_WIDTH // DIFF_HEAD_DIM),
        even_lambda=even_lambda, even_subln=even_subln[:, :, None], even_w_out=even_w_out.astype(BF16),
        mla_w_down=jnp.pad(mla_w_down, ((0, 0), (0, 0), (0, MLA_DOWN_PAD - MLA_DOWN))).astype(BF16),
        mla_qlg=row(mla_q_lat_norm), mla_kvlg=row(mla_kv_lat_norm),
        mla_w_uq=jnp.concatenate([wuq[..., :dn].reshape(n_odd, MLA_Q_RANK, nh * dn),
                                  wuq[..., dn:].reshape(n_odd, MLA_Q_RANK, nh * dr)], axis=-1).astype(BF16),
        mla_w_ukv=jnp.concatenate([wukv[..., :dn].reshape(n_odd, MLA_KV_RANK, nh * dn),
                                   wukv[..., dn:].reshape(n_odd, MLA_KV_RANK, nh * MLA_V)], axis=-1).astype(BF16),
        mla_qgn=tile(mla_q_norm[:, :dn], nh), mla_qgr=tile(mla_q_norm[:, dn:], nh),
        mla_kgn=row(mla_k_norm[:, :dn]),
        mla_kgr=row(jnp.pad(mla_k_norm[:, dn:], ((0, 0), (0, LANES - dr)))),
        mla_w_o=mla_w_o.astype(BF16),
        even_bound=jax.vmap(functools.partial(_score_bound, DIFF_HEAD_DIM, DIFF_HEAD_DIM ** -0.5))(
            even_q_norm, even_k_norm),
        mla_bound=jax.vmap(functools.partial(_score_bound, MLA_QK, MLA_QK ** -0.5))(mla_q_norm, mla_k_norm),
    )


def _trunk(x, p):
    b, s, d = x.shape
    depth = p['ffn1_norm'].shape[0]
    tables = _rope_tables(s)
    flat = lambda a: a.reshape(b * s, a.shape[-1])
    x = flat(x)
    for l in range(depth):
        i = l // 2
        x = _ffn(x, p['ffn1_norm'][l], p['ffn1_w_in'][l], p['ffn1_w_out'][l])
        x3 = x.reshape(b, s, d)
        if l % 2 == 0:
            lambda_init = 0.8 - 0.6 * math.exp(-0.3 * l)
            ya, qT, k, vT = _even_in(x3, p['mix_norm'][l], p['even_w_in'][i], p['even_conv_w'][i],
                                     p['even_qg'][i], p['even_kg'][i], tables)
            yb = lax.cond(
                p['even_bound'][i] <= SCORE_BOUND,
                functools.partial(_diff_attn, lambda_init=lambda_init, bounded=True),
                functools.partial(_diff_attn, lambda_init=lambda_init, bounded=False),
                qT, k, vT, p['even_lambda'][i], p['even_subln'][i])
            x = _out_proj(x, [flat(ya), flat(yb)], p['even_w_out'][i])
        else:
            qT, k, vT = _mla_in(x3, p['mix_norm'][l], p['mla_w_down'][i], p['mla_qlg'][i], p['mla_kvlg'][i],
                                p['mla_w_uq'][i], p['mla_w_ukv'][i], p['mla_qgn'][i], p['mla_qgr'][i],
                                p['mla_kgn'][i], p['mla_kgr'][i], tables)
            o = lax.cond(p['mla_bound'][i] <= SCORE_BOUND,
                         functools.partial(_mla_attn, bounded=True),
                         functools.partial(_mla_attn, bounded=False), qT, k, vT)
            x = _out_proj(x, [flat(o)], p['mla_w_o'][i])
        x = _ffn(x, p['ffn2_norm'][l], p['ffn2_w_in'][l], p['ffn2_w_out'][l])
    return x.reshape(b, s, d)


def kernel(x_prompt, x_sample, ffn1_norm, ffn1_w_in, ffn1_w_out, mix_norm, ffn2_norm, ffn2_w_in, ffn2_w_out, even_w_in, even_conv_w, even_q_norm, even_k_norm, even_lambda, even_subln, even_w_out, mla_w_down, mla_q_lat_norm, mla_kv_lat_norm, mla_w_uq, mla_w_ukv, mla_q_norm, mla_k_norm, mla_w_o):
    p = _prep_weights(ffn1_norm, ffn1_w_in, ffn1_w_out, mix_norm, ffn2_norm, ffn2_w_in, ffn2_w_out,
                      even_w_in, even_conv_w, even_q_norm, even_k_norm, even_lambda, even_subln, even_w_out,
                      mla_w_down, mla_q_lat_norm, mla_kv_lat_norm, mla_w_uq, mla_w_ukv, mla_q_norm, mla_k_norm,
                      mla_w_o)
    return (_trunk(x_prompt, p), _trunk(x_sample, p))
```

```python
import functools
import math

import jax
import jax.numpy as jnp
from jax import lax
from jax.experimental import pallas as pl
from jax.experimental.pallas import tpu as pltpu

F32 = jnp.float32
BF16 = jnp.bfloat16

NORM_EPS = 1e-6
ROPE_THETA = 10000.0
LOG2E = math.log2(math.e)
SCORE_BOUND = 60.0

D_MODEL = 1024
D_FF = 2816
CONV_WIDTH = 512
DIFF_WIDTH = 512
DIFF_HEADS = 4
DIFF_HEAD_DIM = 64
EVEN_IN = 3 * CONV_WIDTH + 3 * DIFF_WIDTH
MLA_HEADS = 8
MLA_NOPE = 128
MLA_ROPE = 64
MLA_V = 128
MLA_QK = MLA_NOPE + MLA_ROPE
MLA_Q_RANK = 384
MLA_KV_RANK = 256
MLA_DOWN = MLA_Q_RANK + MLA_KV_RANK + MLA_ROPE
MLA_DOWN_PAD = 768

LANES = 128
HALO_ROWS = 8
VMEM_LIMIT = 56 * 1024 * 1024

TILES = dict(ffn=512, tok=512, tq=2048, tq_mla=2048, tkc=512, unroll=8)


def _tile(name, n):
    t = min(TILES[name], n)
    assert n % t == 0, (name, n, t)
    return t


def _params(*sem):
    return pltpu.CompilerParams(dimension_semantics=sem, vmem_limit_bytes=VMEM_LIMIT)


def _const_spec(shape):
    nd = len(shape)
    return pl.BlockSpec(shape, lambda *_: (0,) * nd, pipeline_mode=pl.Buffered(1))


def _rms(x, g):
    ms = jnp.mean(x * x, axis=-1, keepdims=True)
    return x * lax.rsqrt(ms + NORM_EPS) * g


def _dot(a, b):
    return jnp.dot(a, b, preferred_element_type=F32)


def _ffn_kernel(x_ref, g_ref, win_ref, wout_ref, o_ref):
    x = x_ref[...]
    xn = _rms(x, g_ref[...]).astype(BF16)
    h = _dot(xn, win_ref[...])
    gate, up = h[:, :D_FF], h[:, D_FF:]
    act = (gate / (1.0 + jnp.exp(-gate)) * up).astype(BF16)
    o_ref[...] = x + 0.5 * _dot(act, wout_ref[...])


def _ffn(x, g, w_in, w_out):
    t, d = x.shape
    tm = _tile('ffn', t)
    return pl.pallas_call(
        _ffn_kernel,
        grid=(t // tm,),
        in_specs=[pl.BlockSpec((tm, d), lambda i: (i, 0)),
                  _const_spec((1, d)),
                  _const_spec(w_in.shape),
                  _const_spec(w_out.shape)],
        out_specs=pl.BlockSpec((tm, d), lambda i: (i, 0)),
        out_shape=jax.ShapeDtypeStruct((t, d), F32),
        compiler_params=_params("parallel"),
        name="ffn",
    )(x, g, w_in, w_out)


def _seg64_rsqrt(x):
    n = x.shape[0]
    lo = lax.broadcasted_iota(jnp.int32, (n, LANES), 1) < 64
    outs = []
    for p in range(x.shape[1] // LANES):
        x2 = x[:, LANES * p:LANES * (p + 1)]
        x2 = x2 * x2
        s_lo = jnp.sum(jnp.where(lo, x2, 0.0), axis=-1, keepdims=True)
        s_hi = jnp.sum(jnp.where(lo, 0.0, x2), axis=-1, keepdims=True)
        outs.append(lax.rsqrt(jnp.where(lo, s_lo, s_hi) * (1.0 / 64) + NORM_EPS))
    return jnp.concatenate(outs, axis=-1)


def _rope_lanes(x, cos, sa, sb):
    w = x.shape[1]
    rep = w // LANES
    if rep > 1:
        cos, sa, sb = (jnp.concatenate([t] * rep, axis=1) for t in (cos, sa, sb))
    return x * cos + pltpu.roll(x, w - 32, 1) * sa + pltpu.roll(x, 32, 1) * sb


def _rope_tables(s):
    d = 64
    inv = 1.0 / (ROPE_THETA ** (jnp.arange(0, d, 2, dtype=F32) / d))
    ang = jnp.arange(s, dtype=F32)[:, None] * inv[None, :]
    cos, sin = jnp.cos(ang), jnp.sin(ang)
    z = jnp.zeros_like(sin)
    cos = jnp.concatenate([cos, cos] * 2, axis=1)
    sa = jnp.concatenate([-sin, z] * 2, axis=1)
    sb = jnp.concatenate([z, sin] * 2, axis=1)
    return cos, sa, sb


def _even_in_kernel(x_ref, xp_ref, xn_ref, g_ref, w_ref, cw_ref, qg_ref, kg_ref, cos_ref, sa_ref, sb_ref,
                    ya_ref, qT_ref, k_ref, vT_ref):
    i = pl.program_id(1)
    n = pl.num_programs(1)
    ts = x_ref.shape[0]
    c1, c2, c3 = CONV_WIDTH, 2 * CONV_WIDTH, 3 * CONV_WIDTH
    g = g_ref[...]
    xn = _rms(x_ref[...], g).astype(BF16)
    proj = _dot(xn, w_ref[...])
    z = proj[:, c1:c2] * proj[:, c2:c3]
    xh = jnp.concatenate([xp_ref[...], xn_ref[...]], axis=0)
    ph = _dot(_rms(xh, g).astype(BF16), w_ref[:, c1:c3])
    zh = ph[:, :c1] * ph[:, c1:]
    z_before = jnp.where(i > 0, zh[HALO_ROWS - 1:HALO_ROWS], 0.0)
    z_after = jnp.where(i < n - 1, zh[HALO_ROWS:HALO_ROWS + 1], 0.0)
    row = lax.broadcasted_iota(jnp.int32, (ts, c1), 0)
    z_m1 = jnp.where(row == 0, z_before, pltpu.roll(z, 1, 0))
    z_p1 = jnp.where(row == ts - 1, z_after, pltpu.roll(z, ts - 1, 0))
    cw = cw_ref[...]
    conv = cw[0:1] * z_m1 + cw[1:2] * z + cw[2:3] * z_p1
    ya_ref[...] = (proj[:, :c1] * conv).astype(BF16)

    cos, sa, sb = cos_ref[...], sa_ref[...], sb_ref[...]
    q = proj[:, c3:c3 + DIFF_WIDTH]
    q = _rope_lanes(q * _seg64_rsqrt(q) * qg_ref[...], cos, sa, sb) * (DIFF_HEAD_DIM ** -0.5 * LOG2E)
    qT_ref[...] = q.T.astype(BF16)
    k = proj[:, c3 + DIFF_WIDTH:c3 + 2 * DIFF_WIDTH]
    k_ref[...] = _rope_lanes(k * _seg64_rsqrt(k) * kg_ref[...], cos, sa, sb).astype(BF16)
    vT_ref[...] = proj[:, c3 + 2 * DIFF_WIDTH:].T.astype(BF16)


def _even_in(x, g, w_in, conv_w, qg, kg, tables):
    b, s, d = x.shape
    ts = _tile('tok', s)
    hb = ts // HALO_ROWS
    nhb = s // HALO_ROWS
    tok = lambda w: pl.BlockSpec((None, ts, w), lambda bi, i: (bi, i, 0))
    tokT = lambda w: pl.BlockSpec((None, w, ts), lambda bi, i: (bi, 0, i))
    tab = pl.BlockSpec((ts, LANES), lambda bi, i: (i, 0))
    return pl.pallas_call(
        _even_in_kernel,
        grid=(b, s // ts),
        in_specs=[tok(d),
                  pl.BlockSpec((None, HALO_ROWS, d), lambda bi, i: (bi, jnp.maximum(i * hb - 1, 0), 0)),
                  pl.BlockSpec((None, HALO_ROWS, d), lambda bi, i: (bi, jnp.minimum((i + 1) * hb, nhb - 1), 0)),
                  _const_spec((1, d)), _const_spec(w_in.shape), _const_spec(conv_w.shape),
                  _const_spec(qg.shape), _const_spec(kg.shape), tab, tab, tab],
        out_specs=[tok(CONV_WIDTH), tokT(DIFF_WIDTH), tok(DIFF_WIDTH), tokT(DIFF_WIDTH)],
        out_shape=[jax.ShapeDtypeStruct((b, s, CONV_WIDTH), BF16),
                   jax.ShapeDtypeStruct((b, DIFF_WIDTH, s), BF16),
                   jax.ShapeDtypeStruct((b, s, DIFF_WIDTH), BF16),
                   jax.ShapeDtypeStruct((b, DIFF_WIDTH, s), BF16)],
        compiler_params=_params("parallel", "parallel"),
        name="even_in",
    )(x, x, x, g, w_in, conv_w, qg, kg, *tables)


def _flash_sweep(qmat, k_ref, vT_ref, tkc):
    nk = k_ref.shape[0] // tkc
    w = qmat.shape[1]
    dv = vT_ref.shape[0]

    def scores(c):
        off = pl.multiple_of(c * tkc, tkc)
        return _dot(k_ref[pl.ds(off, tkc), :], qmat)

    def consume(c, m, l, acc, s):
        off = pl.multiple_of(c * tkc, tkc)
        m_new = jnp.maximum(m, jnp.max(s, axis=0, keepdims=True))
        alpha = jnp.exp2(m - m_new)
        p = jnp.exp2(s - m_new)
        l = alpha * l + jnp.sum(p, axis=0, keepdims=True)
        acc = alpha * acc + _dot(vT_ref[:, pl.ds(off, tkc)], p.astype(BF16))
        return m_new, l, acc

    def step(c, carry):
        m, l, acc, s = carry
        s_next = scores(c + 1)
        return consume(c, m, l, acc, s) + (s_next,)

    init = (jnp.full((1, w), -jnp.inf, F32), jnp.zeros((1, w), F32), jnp.zeros((dv, w), F32), scores(0))
    m, l, acc, s = lax.fori_loop(0, nk - 1, step, init)
    _, l, acc = consume(nk - 1, m, l, acc, s)
    return l, acc


def _bounded_sweep(qmat, k_ref, vT_ref, l_ref, acc_ref, tkc):
    nk = k_ref.shape[0] // tkc
    l_ref[...] = jnp.zeros(l_ref.shape, F32)
    acc_ref[...] = jnp.zeros(acc_ref.shape, F32)

    def step(c, carry):
        off = pl.multiple_of(c * tkc, tkc)
        p = jnp.exp2(_dot(k_ref[pl.ds(off, tkc), :], qmat))
        l_ref[...] += jnp.sum(p, axis=0, keepdims=True)
        acc_ref[...] += _dot(vT_ref[:, pl.ds(off, tkc)], p.astype(BF16))
        return carry

    lax.fori_loop(0, nk, step, 0, unroll=min(nk, TILES['unroll']))
    return l_ref[...], acc_ref[...]


def _sweep(bounded, qmat, k_ref, vT_ref, l_ref, acc_ref, tkc):
    if bounded:
        return _bounded_sweep(qmat, k_ref, vT_ref, l_ref, acc_ref, tkc)
    return _flash_sweep(qmat, k_ref, vT_ref, tkc)


def _score_bound(d, scale, qg, kg):
    margin = 1.02
    return margin * scale * LOG2E * d * jnp.max(jnp.abs(qg)) * jnp.max(jnp.abs(kg))


def _diff_attn_kernel(qT_ref, k_ref, vT_ref, lam_ref, sg_ref, o_ref, qbd_ref, l_ref, acc_ref,
                      *, tkc, lambda_init, bounded):
    tq = qT_ref.shape[1]
    dh = DIFF_HEAD_DIM
    qbd_ref[...] = jnp.zeros(qbd_ref.shape, BF16)
    qbd_ref[0:dh, 0:tq] = qT_ref[0:dh, :]
    qbd_ref[dh:2 * dh, tq:2 * tq] = qT_ref[dh:2 * dh, :]
    l, acc = _sweep(bounded, qbd_ref[...], k_ref, vT_ref, l_ref, acc_ref, tkc)
    o = acc / l
    lv = lam_ref[...]
    lam = (jnp.exp(jnp.sum(lv[0:1] * lv[1:2], axis=-1, keepdims=True))
           - jnp.exp(jnp.sum(lv[2:3] * lv[3:4], axis=-1, keepdims=True)) + lambda_init)
    od = o[:, :tq] - lam * o[:, tq:]
    ms = jnp.mean(od * od, axis=0, keepdims=True)
    on = od * lax.rsqrt(ms + NORM_EPS) * sg_ref[...] * (1.0 - lambda_init)
    o_ref[...] = on.T.astype(o_ref.dtype)


def _diff_attn(qT, k, vT, lam_vecs, subln_g, lambda_init, bounded):
    b, w, s = qT.shape
    tq, tkc = _tile('tq', s), _tile('tkc', s)
    hw = 2 * DIFF_HEAD_DIM
    return pl.pallas_call(
        functools.partial(_diff_attn_kernel, tkc=tkc, lambda_init=lambda_init, bounded=bounded),
        grid=(b, DIFF_HEADS, s // tq),
        in_specs=[pl.BlockSpec((None, hw, tq), lambda bi, h, i: (bi, h, i)),
                  pl.BlockSpec((None, s, hw), lambda bi, h, i: (bi, 0, h)),
                  pl.BlockSpec((None, hw, s), lambda bi, h, i: (bi, h, 0)),
                  _const_spec(lam_vecs.shape), _const_spec(subln_g.shape)],
        out_specs=pl.BlockSpec((None, tq, hw), lambda bi, h, i: (bi, i, h)),
        out_shape=jax.ShapeDtypeStruct((b, s, w), BF16),
        scratch_shapes=[pltpu.VMEM((hw, 2 * tq), BF16), pltpu.VMEM((1, 2 * tq), F32),
                        pltpu.VMEM((hw, 2 * tq), F32)],
        compiler_params=_params("parallel", "parallel", "arbitrary"),
        name="diff_attn_bounded" if bounded else "diff_attn",
    )(qT, k, vT, lam_vecs, subln_g)


def _mla_attn_kernel(qT_ref, k_ref, vT_ref, o_ref, l_ref, acc_ref, *, tkc, bounded):
    l, acc = _sweep(bounded, qT_ref[...], k_ref, vT_ref, l_ref, acc_ref, tkc)
    o_ref[...] = (acc / l).T.astype(o_ref.dtype)


def _mla_attn(qT, k, vT, bounded):
    b, nh, dk, s = qT.shape
    tq, tkc = _tile('tq_mla', s), _tile('tkc', s)
    return pl.pallas_call(
        functools.partial(_mla_attn_kernel, tkc=tkc, bounded=bounded),
        grid=(b, nh, s // tq),
        in_specs=[pl.BlockSpec((None, None, dk, tq), lambda bi, h, i: (bi, h, 0, i)),
                  pl.BlockSpec((None, None, s, dk), lambda bi, h, i: (bi, h, 0, 0)),
                  pl.BlockSpec((None, MLA_V, s), lambda bi, h, i: (bi, h, 0))],
        out_specs=pl.BlockSpec((None, tq, MLA_V), lambda bi, h, i: (bi, i, h)),
        out_shape=jax.ShapeDtypeStruct((b, s, nh * MLA_V), BF16),
        scratch_shapes=[pltpu.VMEM((1, tq), F32), pltpu.VMEM((MLA_V, tq), F32)],
        compiler_params=_params("parallel", "parallel", "arbitrary"),
        name="mla_attn_bounded" if bounded else "mla_attn",
    )(qT, k, vT)


def _out2_kernel(x_ref, a_ref, b_ref, w_ref, o_ref):
    ka = a_ref.shape[1]
    o_ref[...] = x_ref[...] + _dot(a_ref[...], w_ref[:ka, :]) + _dot(b_ref[...], w_ref[ka:, :])


def _out1_kernel(x_ref, a_ref, w_ref, o_ref):
    o_ref[...] = x_ref[...] + _dot(a_ref[...], w_ref[...])


def _out_proj(x, acts, w):
    t, d = x.shape
    tm = _tile('tok', t)
    row = lambda wd: pl.BlockSpec((tm, wd), lambda i: (i, 0))
    return pl.pallas_call(
        _out2_kernel if len(acts) == 2 else _out1_kernel,
        grid=(t // tm,),
        in_specs=[row(d)] + [row(a.shape[1]) for a in acts] + [_const_spec(w.shape)],
        out_specs=row(d),
        out_shape=jax.ShapeDtypeStruct((t, d), F32),
        compiler_params=_params("parallel"),
        name="out_proj",
    )(x, *acts, w)


def _mla_in_kernel(x_ref, g_ref, wd_ref, qlg_ref, kvlg_ref, wuq_ref, wukv_ref, qgn_ref, qgr_ref, kgn_ref, kgr_ref,
                   cos_ref, sa_ref, sb_ref, qT_ref, k_ref, vT_ref):
    nh, dn, dr = MLA_HEADS, MLA_NOPE, MLA_ROPE
    ts = x_ref.shape[0]
    cos, sa, sb = cos_ref[...], sa_ref[...], sb_ref[...]
    xn = _rms(x_ref[...], g_ref[...]).astype(BF16)
    lat = _dot(xn, wd_ref[...])
    cq = _rms(lat[:, :MLA_Q_RANK], qlg_ref[...]).astype(BF16)
    ckv = _rms(lat[:, MLA_Q_RANK:MLA_Q_RANK + MLA_KV_RANK], kvlg_ref[...]).astype(BF16)
    kr = lat[:, MLA_Q_RANK + MLA_KV_RANK:]
    q = _dot(cq, wuq_ref[...])
    kv = _dot(ckv, wukv_ref[...])
    lo = lax.broadcasted_iota(jnp.int32, (ts, LANES), 1) < dr
    inv_d = 1.0 / MLA_QK
    scale = MLA_QK ** -0.5 * LOG2E

    def ssq(a):
        return jnp.sum(a * a, axis=-1, keepdims=True)

    qn_parts, qr_scale = [], []
    for pair in range(nh // 2):
        qr2 = q[:, nh * dn + LANES * pair:nh * dn + LANES * (pair + 1)]
        qr2 = qr2 * qr2
        s_even = jnp.sum(jnp.where(lo, qr2, 0.0), axis=-1, keepdims=True)
        s_odd = jnp.sum(jnp.where(lo, 0.0, qr2), axis=-1, keepdims=True)
        rs = []
        for h, s_r in ((2 * pair, s_even), (2 * pair + 1, s_odd)):
            qn_h = q[:, h * dn:(h + 1) * dn]
            r = lax.rsqrt((ssq(qn_h) + s_r) * inv_d + NORM_EPS)
            qn_parts.append(qn_h * r)
            rs.append(r)
        qr_scale.append(jnp.where(lo, rs[0], rs[1]))
    qn = jnp.concatenate(qn_parts, axis=1) * qgn_ref[...] * scale
    qr = q[:, nh * dn:] * jnp.concatenate(qr_scale, axis=1) * qgr_ref[...]
    qr = _rope_lanes(qr, cos, sa, sb) * scale
    qnT = qn.T.astype(BF16)
    qrT = qr.T.astype(BF16)
    for h in range(nh):
        qT_ref[h, 0:dn, :] = qnT[h * dn:(h + 1) * dn, :]
        qT_ref[h, dn:dn + dr, :] = qrT[h * dr:(h + 1) * dr, :]

    kr_ssq = ssq(kr)
    kr_rot = _rope_lanes(kr * kgr_ref[...], cos, sa, sb)
    kgn = kgn_ref[...]
    for h in range(nh):
        kn_h = kv[:, h * dn:(h + 1) * dn]
        r = lax.rsqrt((ssq(kn_h) + kr_ssq) * inv_d + NORM_EPS)
        k_ref[h, :, 0:dn] = (kn_h * r * kgn).astype(BF16)
        k_ref[h, :, dn:dn + dr] = (kr_rot * r)[:, :dr].astype(BF16)
    vT_ref[...] = kv[:, nh * dn:].T.astype(BF16)


def _mla_in(x, g, wd, qlg, kvlg, wuq, wukv, qgn, qgr, kgn, kgr, tables):
    b, s, d = x.shape
    ts = _tile('tok', s)
    nh = MLA_HEADS
    tab = pl.BlockSpec((ts, LANES), lambda bi, i: (i, 0))
    consts = [g, wd, qlg, kvlg, wuq, wukv, qgn, qgr, kgn, kgr]
    return pl.pallas_call(
        _mla_in_kernel,
        grid=(b, s // ts),
        in_specs=[pl.BlockSpec((None, ts, d), lambda bi, i: (bi, i, 0))]
                 + [_const_spec(c.shape) for c in consts] + [tab, tab, tab],
        out_specs=[pl.BlockSpec((None, nh, MLA_QK, ts), lambda bi, i: (bi, 0, 0, i)),
                   pl.BlockSpec((None, nh, ts, MLA_QK), lambda bi, i: (bi, 0, i, 0)),
                   pl.BlockSpec((None, nh * MLA_V, ts), lambda bi, i: (bi, 0, i))],
        out_shape=[jax.ShapeDtypeStruct((b, nh, MLA_QK, s), BF16),
                   jax.ShapeDtypeStruct((b, nh, s, MLA_QK), BF16),
                   jax.ShapeDtypeStruct((b, nh * MLA_V, s), BF16)],
        compiler_params=_params("parallel", "parallel"),
        name="mla_in",
    )(x, *consts, *tables)


def _prep_weights(ffn1_norm, ffn1_w_in, ffn1_w_out, mix_norm, ffn2_norm, ffn2_w_in, ffn2_w_out,
                  even_w_in, even_conv_w, even_q_norm, even_k_norm, even_lambda, even_subln, even_w_out,
                  mla_w_down, mla_q_lat_norm, mla_kv_lat_norm, mla_w_uq, mla_w_ukv, mla_q_norm, mla_k_norm, mla_w_o):
    nh, dn, dr = MLA_HEADS, MLA_NOPE, MLA_ROPE
    n_odd = mla_w_down.shape[0]
    row = lambda a: a[:, None, :]
    wuq = mla_w_uq.reshape(n_odd, MLA_Q_RANK, nh, MLA_QK)
    wukv = mla_w_ukv.reshape(n_odd, MLA_KV_RANK, nh, dn + MLA_V)
    tile = lambda a, k: jnp.tile(a, (1, k))[:, None, :]
    return dict(
        ffn1_norm=row(ffn1_norm), ffn1_w_in=ffn1_w_in.astype(BF16), ffn1_w_out=ffn1_w_out.astype(BF16),
        mix_norm=row(mix_norm),
        ffn2_norm=row(ffn2_norm), ffn2_w_in=ffn2_w_in.astype(BF16), ffn2_w_out=ffn2_w_out.astype(BF16),
        even_w_in=even_w_in.astype(BF16), even_conv_w=even_conv_w,
        even_qg=tile(even_q_norm, DIFF_WIDTH // DIFF_HEAD_DIM), even_kg=tile(even_k_norm, DIFF_WIDTH // DIFF_HEAD_DIM),
        even_lambda=even_lambda, even_subln=even_subln[:, :, None], even_w_out=even_w_out.astype(BF16),
        mla_w_down=jnp.pad(mla_w_down, ((0, 0), (0, 0), (0, MLA_DOWN_PAD - MLA_DOWN))).astype(BF16),
        mla_qlg=row(mla_q_lat_norm), mla_kvlg=row(mla_kv_lat_norm),
        mla_w_uq=jnp.concatenate([wuq[..., :dn].reshape(n_odd, MLA_Q_RANK, nh * dn),
                                  wuq[..., dn:].reshape(n_odd, MLA_Q_RANK, nh * dr)], axis=-1).astype(BF16),
        mla_w_ukv=jnp.concatenate([wukv[..., :dn].reshape(n_odd, MLA_KV_RANK, nh * dn),
                                   wukv[..., dn:].reshape(n_odd, MLA_KV_RANK, nh * MLA_V)], axis=-1).astype(BF16),
        mla_qgn=tile(mla_q_norm[:, :dn], nh), mla_qgr=tile(mla_q_norm[:, dn:], nh),
        mla_kgn=row(mla_k_norm[:, :dn]),
        mla_kgr=row(jnp.pad(mla_k_norm[:, dn:], ((0, 0), (0, LANES - dr)))),
        mla_w_o=mla_w_o.astype(BF16),
        even_bound=jax.vmap(functools.partial(_score_bound, DIFF_HEAD_DIM, DIFF_HEAD_DIM ** -0.5))(
            even_q_norm, even_k_norm),
        mla_bound=jax.vmap(functools.partial(_score_bound, MLA_QK, MLA_QK ** -0.5))(mla_q_norm, mla_k_norm),
    )


def _trunk(x, p):
    b, s, d = x.shape
    depth = p['ffn1_norm'].shape[0]
    tables = _rope_tables(s)
    flat = lambda a: a.reshape(b * s, a.shape[-1])
    x = flat(x)
    for l in range(depth):
        i = l // 2
        x = _ffn(x, p['ffn1_norm'][l], p['ffn1_w_in'][l], p['ffn1_w_out'][l])
        x3 = x.reshape(b, s, d)
        if l % 2 == 0:
            lambda_init = 0.8 - 0.6 * math.exp(-0.3 * l)
            ya, qT, k, vT = _even_in(x3, p['mix_norm'][l], p['even_w_in'][i], p['even_conv_w'][i],
                                     p['even_qg'][i], p['even_kg'][i], tables)
            yb = lax.cond(
                p['even_bound'][i] <= SCORE_BOUND,
                functools.partial(_diff_attn, lambda_init=lambda_init, bounded=True),
                functools.partial(_diff_attn, lambda_init=lambda_init, bounded=False),
                qT, k, vT, p['even_lambda'][i], p['even_subln'][i])
            x = _out_proj(x, [flat(ya), flat(yb)], p['even_w_out'][i])
        else:
            qT, k, vT = _mla_in(x3, p['mix_norm'][l], p['mla_w_down'][i], p['mla_qlg'][i], p['mla_kvlg'][i],
                                p['mla_w_uq'][i], p['mla_w_ukv'][i], p['mla_qgn'][i], p['mla_qgr'][i],
                                p['mla_kgn'][i], p['mla_kgr'][i], tables)
            o = lax.cond(p['mla_bound'][i] <= SCORE_BOUND,
                         functools.partial(_mla_attn, bounded=True),
                         functools.partial(_mla_attn, bounded=False), qT, k, vT)
            x = _out_proj(x, [flat(o)], p['mla_w_o'][i])
        x = _ffn(x, p['ffn2_norm'][l], p['ffn2_w_in'][l], p['ffn2_w_out'][l])
    return x.reshape(b, s, d)


def kernel(x_prompt, x_sample, ffn1_norm, ffn1_w_in, ffn1_w_out, mix_norm, ffn2_norm, ffn2_w_in, ffn2_w_out, even_w_in, even_conv_w, even_q_norm, even_k_norm, even_lambda, even_subln, even_w_out, mla_w_down, mla_q_lat_norm, mla_kv_lat_norm, mla_w_uq, mla_w_ukv, mla_q_norm, mla_k_norm, mla_w_o):
    p = _prep_weights(ffn1_norm, ffn1_w_in, ffn1_w_out, mix_norm, ffn2_norm, ffn2_w_in, ffn2_w_out,
                      even_w_in, even_conv_w, even_q_norm, even_k_norm, even_lambda, even_subln, even_w_out,
                      mla_w_down, mla_q_lat_norm, mla_kv_lat_norm, mla_w_uq, mla_w_ukv, mla_q_norm, mla_k_norm,
                      mla_w_o)
    return (_trunk(x_prompt, p), _trunk(x_sample, p))
```

```python
import functools
import math

import jax
import jax.numpy as jnp
from jax import lax
from jax.experimental import pallas as pl
from jax.experimental.pallas import tpu as pltpu

F32 = jnp.float32
BF16 = jnp.bfloat16

NORM_EPS = 1e-6
ROPE_THETA = 10000.0
LOG2E = math.log2(math.e)
SCORE_BOUND = 60.0

D_MODEL = 1024
D_FF = 2816
CONV_WIDTH = 512
DIFF_WIDTH = 512
DIFF_HEADS = 4
DIFF_HEAD_DIM = 64
EVEN_IN = 3 * CONV_WIDTH + 3 * DIFF_WIDTH
MLA_HEADS = 8
MLA_NOPE = 128
MLA_ROPE = 64
MLA_V = 128
MLA_QK = MLA_NOPE + MLA_ROPE
MLA_Q_RANK = 384
MLA_KV_RANK = 256
MLA_DOWN = MLA_Q_RANK + MLA_KV_RANK + MLA_ROPE
MLA_DOWN_PAD = 768

LANES = 128
HALO_ROWS = 8
VMEM_LIMIT = 56 * 1024 * 1024

TILES = dict(ffn=512, tok=512, tq=2048, tq_mla=2048, tkc=512, unroll=8)


def _tile(name, n):
    t = min(TILES[name], n)
    assert n % t == 0, (name, n, t)
    return t


def _params(*sem):
    return pltpu.CompilerParams(dimension_semantics=sem, vmem_limit_bytes=VMEM_LIMIT)


def _const_spec(shape):
    nd = len(shape)
    return pl.BlockSpec(shape, lambda *_: (0,) * nd, pipeline_mode=pl.Buffered(1))


def _rms(x, g):
    ms = jnp.mean(x * x, axis=-1, keepdims=True)
    return x * lax.rsqrt(ms + NORM_EPS) * g


def _dot(a, b):
    return jnp.dot(a, b, preferred_element_type=F32)


def _ffn_kernel(*refs, n_mix):
    x_ref, g_ref, win_ref, wout_ref = refs[:4]
    o_ref = refs[-1]
    x = x_ref[...]
    if n_mix:
        wmix_ref = refs[4 + n_mix]
        row = 0
        for a_ref in refs[4:4 + n_mix]:
            x = x + _dot(a_ref[...], wmix_ref[row:row + a_ref.shape[1], :])
            row += a_ref.shape[1]
    xn = _rms(x, g_ref[...]).astype(BF16)
    h = _dot(xn, win_ref[...])
    gate, up = h[:, :D_FF], h[:, D_FF:]
    act = (gate / (1.0 + jnp.exp(-gate)) * up).astype(BF16)
    o_ref[...] = x + 0.5 * _dot(act, wout_ref[...])


def _ffn(x, g, w_in, w_out, mix_acts=(), w_mix=None):
    t, d = x.shape
    tm = _tile('ffn', t)
    row = lambda wd: pl.BlockSpec((tm, wd), lambda i: (i, 0))
    mix_args = list(mix_acts) + ([w_mix] if mix_acts else [])
    mix_specs = [row(a.shape[1]) for a in mix_acts] + ([_const_spec(w_mix.shape)] if mix_acts else [])
    return pl.pallas_call(
        functools.partial(_ffn_kernel, n_mix=len(mix_acts)),
        grid=(t // tm,),
        in_specs=[row(d), _const_spec((1, d)), _const_spec(w_in.shape), _const_spec(w_out.shape)] + mix_specs,
        out_specs=row(d),
        out_shape=jax.ShapeDtypeStruct((t, d), F32),
        compiler_params=_params("parallel"),
        name="mix_ffn" if mix_acts else "ffn",
    )(x, g, w_in, w_out, *mix_args)


def _seg64_rsqrt(x):
    n = x.shape[0]
    lo = lax.broadcasted_iota(jnp.int32, (n, LANES), 1) < 64
    outs = []
    for p in range(x.shape[1] // LANES):
        x2 = x[:, LANES * p:LANES * (p + 1)]
        x2 = x2 * x2
        s_lo = jnp.sum(jnp.where(lo, x2, 0.0), axis=-1, keepdims=True)
        s_hi = jnp.sum(jnp.where(lo, 0.0, x2), axis=-1, keepdims=True)
        outs.append(lax.rsqrt(jnp.where(lo, s_lo, s_hi) * (1.0 / 64) + NORM_EPS))
    return jnp.concatenate(outs, axis=-1)


def _rope_lanes(x, cos, sa, sb):
    w = x.shape[1]
    rep = w // LANES
    if rep > 1:
        cos, sa, sb = (jnp.concatenate([t] * rep, axis=1) for t in (cos, sa, sb))
    return x * cos + pltpu.roll(x, w - 32, 1) * sa + pltpu.roll(x, 32, 1) * sb


def _rope_tables(s):
    d = 64
    inv = 1.0 / (ROPE_THETA ** (jnp.arange(0, d, 2, dtype=F32) / d))
    ang = jnp.arange(s, dtype=F32)[:, None] * inv[None, :]
    cos, sin = jnp.cos(ang), jnp.sin(ang)
    z = jnp.zeros_like(sin)
    cos_l = jnp.concatenate([cos, cos] * 2, axis=1)
    sa = jnp.concatenate([-sin, z] * 2, axis=1)
    sb = jnp.concatenate([z, sin] * 2, axis=1)
    return (cos_l, sa, sb), (cos.T, sin.T)


def _dot_nt(a, b):
    return lax.dot_general(a, b, (((1,), (1,)), ((), ())), preferred_element_type=F32)


def _rope_rows(x, cos_t, sin_t):
    x1, x2 = x[:32], x[32:]
    return x1 * cos_t - x2 * sin_t, x1 * sin_t + x2 * cos_t


def _even_in_kernel(x_ref, xp_ref, xn_ref, g_ref, w_ref, wqv_ref, cw_ref, qg_ref, kg_ref,
                    cos_ref, sa_ref, sb_ref, cost_ref, sint_ref, ya_ref, qT_ref, k_ref, vT_ref):
    i = pl.program_id(1)
    n = pl.num_programs(1)
    ts = x_ref.shape[0]
    c1, c2, c3 = CONV_WIDTH, 2 * CONV_WIDTH, 3 * CONV_WIDTH
    dh = DIFF_HEAD_DIM
    g = g_ref[...]
    xn = _rms(x_ref[...], g).astype(BF16)
    proj = _dot(xn, w_ref[...])
    qv = _dot_nt(wqv_ref[...], xn)
    vT_ref[...] = qv[DIFF_WIDTH:].astype(BF16)
    cos_t, sin_t = cost_ref[...], sint_ref[...]
    for j in range(DIFF_WIDTH // dh):
        blk = qv[j * dh:(j + 1) * dh]
        r = lax.rsqrt(jnp.sum(blk * blk, axis=0, keepdims=True) * (1.0 / dh) + NORM_EPS)
        o1, o2 = _rope_rows(blk * r * qg_ref[j * dh:(j + 1) * dh, :], cos_t, sin_t)
        qT_ref[j * dh:j * dh + dh // 2, :] = o1.astype(BF16)
        qT_ref[j * dh + dh // 2:(j + 1) * dh, :] = o2.astype(BF16)
    z = proj[:, c1:c2] * proj[:, c2:c3]
    xh = jnp.concatenate([xp_ref[...], xn_ref[...]], axis=0)
    ph = _dot(_rms(xh, g).astype(BF16), w_ref[:, c1:c3])
    zh = ph[:, :c1] * ph[:, c1:]
    z_before = jnp.where(i > 0, zh[HALO_ROWS - 1:HALO_ROWS], 0.0)
    z_after = jnp.where(i < n - 1, zh[HALO_ROWS:HALO_ROWS + 1], 0.0)
    row = lax.broadcasted_iota(jnp.int32, (ts, c1), 0)
    z_m1 = jnp.where(row == 0, z_before, pltpu.roll(z, 1, 0))
    z_p1 = jnp.where(row == ts - 1, z_after, pltpu.roll(z, ts - 1, 0))
    cw = cw_ref[...]
    conv = cw[0:1] * z_m1 + cw[1:2] * z + cw[2:3] * z_p1
    ya_ref[...] = (proj[:, :c1] * conv).astype(BF16)

    k = proj[:, c3:]
    k_ref[...] = _rope_lanes(k * _seg64_rsqrt(k) * kg_ref[...], cos_ref[...], sa_ref[...], sb_ref[...]).astype(BF16)


def _even_in(x, g, w_tok, w_qv, conv_w, qg_t, kg, tables, tables_t):
    b, s, d = x.shape
    ts = _tile('tok', s)
    hb = ts // HALO_ROWS
    nhb = s // HALO_ROWS
    tok = lambda w: pl.BlockSpec((None, ts, w), lambda bi, i: (bi, i, 0))
    tokT = lambda w: pl.BlockSpec((None, w, ts), lambda bi, i: (bi, 0, i))
    tab = pl.BlockSpec((ts, LANES), lambda bi, i: (i, 0))
    tab_t = pl.BlockSpec((32, ts), lambda bi, i: (0, i))
    qg_t = jnp.broadcast_to(qg_t, (DIFF_WIDTH, ts))
    return pl.pallas_call(
        _even_in_kernel,
        grid=(b, s // ts),
        in_specs=[tok(d),
                  pl.BlockSpec((None, HALO_ROWS, d), lambda bi, i: (bi, jnp.maximum(i * hb - 1, 0), 0)),
                  pl.BlockSpec((None, HALO_ROWS, d), lambda bi, i: (bi, jnp.minimum((i + 1) * hb, nhb - 1), 0)),
                  _const_spec((1, d)), _const_spec(w_tok.shape), _const_spec(w_qv.shape), _const_spec(conv_w.shape),
                  _const_spec(qg_t.shape), _const_spec(kg.shape), tab, tab, tab, tab_t, tab_t],
        out_specs=[tok(CONV_WIDTH), tokT(DIFF_WIDTH), tok(DIFF_WIDTH), tokT(DIFF_WIDTH)],
        out_shape=[jax.ShapeDtypeStruct((b, s, CONV_WIDTH), BF16),
                   jax.ShapeDtypeStruct((b, DIFF_WIDTH, s), BF16),
                   jax.ShapeDtypeStruct((b, s, DIFF_WIDTH), BF16),
                   jax.ShapeDtypeStruct((b, DIFF_WIDTH, s), BF16)],
        compiler_params=_params("parallel", "parallel"),
        name="even_in",
    )(x, x, x, g, w_tok, w_qv, conv_w, qg_t, kg, *tables, *tables_t)


def _flash_sweep(qmat, k_ref, vT_ref, tkc):
    nk = k_ref.shape[0] // tkc
    w = qmat.shape[1]
    dv = vT_ref.shape[0]

    def scores(c):
        off = pl.multiple_of(c * tkc, tkc)
        return _dot(k_ref[pl.ds(off, tkc), :], qmat)

    def consume(c, m, l, acc, s):
        off = pl.multiple_of(c * tkc, tkc)
        m_new = jnp.maximum(m, jnp.max(s, axis=0, keepdims=True))
        alpha = jnp.exp2(m - m_new)
        p = jnp.exp2(s - m_new)
        l = alpha * l + jnp.sum(p, axis=0, keepdims=True)
        acc = alpha * acc + _dot(vT_ref[:, pl.ds(off, tkc)], p.astype(BF16))
        return m_new, l, acc

    def step(c, carry):
        m, l, acc, s = carry
        s_next = scores(c + 1)
        return consume(c, m, l, acc, s) + (s_next,)

    init = (jnp.full((1, w), -jnp.inf, F32), jnp.zeros((1, w), F32), jnp.zeros((dv, w), F32), scores(0))
    m, l, acc, s = lax.fori_loop(0, nk - 1, step, init)
    _, l, acc = consume(nk - 1, m, l, acc, s)
    return l, acc


def _bounded_sweep(qmat, k_ref, vT_ref, l_ref, acc_ref, tkc):
    nk = k_ref.shape[0] // tkc
    l_ref[...] = jnp.zeros(l_ref.shape, F32)
    acc_ref[...] = jnp.zeros(acc_ref.shape, F32)

    def step(c, carry):
        off = pl.multiple_of(c * tkc, tkc)
        p = jnp.exp2(_dot(k_ref[pl.ds(off, tkc), :], qmat))
        l_ref[...] += jnp.sum(p, axis=0, keepdims=True)
        acc_ref[...] += _dot(vT_ref[:, pl.ds(off, tkc)], p.astype(BF16))
        return carry

    lax.fori_loop(0, nk, step, 0, unroll=min(nk, TILES['unroll']))
    return l_ref[...], acc_ref[...]


def _sweep(bounded, qmat, k_ref, vT_ref, l_ref, acc_ref, tkc):
    if bounded:
        return _bounded_sweep(qmat, k_ref, vT_ref, l_ref, acc_ref, tkc)
    return _flash_sweep(qmat, k_ref, vT_ref, tkc)


def _score_bound(d, scale, qg, kg):
    margin = 1.02
    return margin * scale * LOG2E * d * jnp.max(jnp.abs(qg)) * jnp.max(jnp.abs(kg))


def _diff_attn_kernel(qT_ref, k_ref, vT_ref, lam_ref, sg_ref, o_ref, qbd_ref, l_ref, acc_ref,
                      *, tkc, lambda_init, bounded):
    tq = qT_ref.shape[1]
    dh = DIFF_HEAD_DIM
    qbd_ref[...] = jnp.zeros(qbd_ref.shape, BF16)
    qbd_ref[0:dh, 0:tq] = qT_ref[0:dh, :]
    qbd_ref[dh:2 * dh, tq:2 * tq] = qT_ref[dh:2 * dh, :]
    l, acc = _sweep(bounded, qbd_ref[...], k_ref, vT_ref, l_ref, acc_ref, tkc)
    o = acc / l
    lv = lam_ref[...]
    lam = (jnp.exp(jnp.sum(lv[0:1] * lv[1:2], axis=-1, keepdims=True))
           - jnp.exp(jnp.sum(lv[2:3] * lv[3:4], axis=-1, keepdims=True)) + lambda_init)
    od = o[:, :tq] - lam * o[:, tq:]
    ms = jnp.mean(od * od, axis=0, keepdims=True)
    on = od * lax.rsqrt(ms + NORM_EPS) * sg_ref[...] * (1.0 - lambda_init)
    o_ref[...] = on.T.astype(o_ref.dtype)


def _diff_attn(qT, k, vT, lam_vecs, subln_g, lambda_init, bounded):
    b, w, s = qT.shape
    tq, tkc = _tile('tq', s), _tile('tkc', s)
    hw = 2 * DIFF_HEAD_DIM
    return pl.pallas_call(
        functools.partial(_diff_attn_kernel, tkc=tkc, lambda_init=lambda_init, bounded=bounded),
        grid=(b, DIFF_HEADS, s // tq),
        in_specs=[pl.BlockSpec((None, hw, tq), lambda bi, h, i: (bi, h, i)),
                  pl.BlockSpec((None, s, hw), lambda bi, h, i: (bi, 0, h)),
                  pl.BlockSpec((None, hw, s), lambda bi, h, i: (bi, h, 0)),
                  _const_spec(lam_vecs.shape), _const_spec(subln_g.shape)],
        out_specs=pl.BlockSpec((None, tq, hw), lambda bi, h, i: (bi, i, h)),
        out_shape=jax.ShapeDtypeStruct((b, s, w), BF16),
        scratch_shapes=[pltpu.VMEM((hw, 2 * tq), BF16), pltpu.VMEM((1, 2 * tq), F32),
                        pltpu.VMEM((hw, 2 * tq), F32)],
        compiler_params=_params("parallel", "parallel", "arbitrary"),
        name="diff_attn_bounded" if bounded else "diff_attn",
    )(qT, k, vT, lam_vecs, subln_g)


def _mla_attn_kernel(qT_ref, k_ref, vT_ref, o_ref, l_ref, acc_ref, *, tkc, bounded):
    l, acc = _sweep(bounded, qT_ref[...], k_ref, vT_ref, l_ref, acc_ref, tkc)
    o_ref[...] = (acc / l).T.astype(o_ref.dtype)


def _mla_attn(qT, k, vT, bounded):
    b, nh, dk, s = qT.shape
    tq, tkc = _tile('tq_mla', s), _tile('tkc', s)
    return pl.pallas_call(
        functools.partial(_mla_attn_kernel, tkc=tkc, bounded=bounded),
        grid=(b, nh, s // tq),
        in_specs=[pl.BlockSpec((None, None, dk, tq), lambda bi, h, i: (bi, h, 0, i)),
                  pl.BlockSpec((None, None, s, dk), lambda bi, h, i: (bi, h, 0, 0)),
                  pl.BlockSpec((None, MLA_V, s), lambda bi, h, i: (bi, h, 0))],
        out_specs=pl.BlockSpec((None, tq, MLA_V), lambda bi, h, i: (bi, i, h)),
        out_shape=jax.ShapeDtypeStruct((b, s, nh * MLA_V), BF16),
        scratch_shapes=[pltpu.VMEM((1, tq), F32), pltpu.VMEM((MLA_V, tq), F32)],
        compiler_params=_params("parallel", "parallel", "arbitrary"),
        name="mla_attn_bounded" if bounded else "mla_attn",
    )(qT, k, vT)


def _mla_in_kernel(x_ref, g_ref, wd_ref, qlg_ref, kvlg_ref, wuq_ref, wuk_ref, wuv_ref, qg_ref, kgn_ref, kgr_ref,
                   cos_ref, sa_ref, sb_ref, cost_ref, sint_ref, qT_ref, k_ref, vT_ref):
    nh, dn, dr = MLA_HEADS, MLA_NOPE, MLA_ROPE
    inv_d = 1.0 / MLA_QK
    xn = _rms(x_ref[...], g_ref[...]).astype(BF16)
    lat = _dot(xn, wd_ref[...])
    cq = _rms(lat[:, :MLA_Q_RANK], qlg_ref[...]).astype(BF16)
    ckv = _rms(lat[:, MLA_Q_RANK:MLA_Q_RANK + MLA_KV_RANK], kvlg_ref[...]).astype(BF16)
    kr = lat[:, MLA_Q_RANK + MLA_KV_RANK:]
    vT_ref[...] = _dot_nt(wuv_ref[...], ckv).astype(BF16)

    qT = _dot_nt(wuq_ref[...], cq)
    qg, cos_t, sin_t = qg_ref[...], cost_ref[...], sint_ref[...]
    for h in range(nh):
        blk = qT[h * MLA_QK:(h + 1) * MLA_QK]
        r = lax.rsqrt(jnp.sum(blk * blk, axis=0, keepdims=True) * inv_d + NORM_EPS)
        blk = blk * r * qg
        o1, o2 = _rope_rows(blk[dn:], cos_t, sin_t)
        qT_ref[h, 0:dn, :] = blk[:dn].astype(BF16)
        qT_ref[h, dn:dn + dr // 2, :] = o1.astype(BF16)
        qT_ref[h, dn + dr // 2:, :] = o2.astype(BF16)

    def ssq(a):
        return jnp.sum(a * a, axis=-1, keepdims=True)

    kn = _dot(ckv, wuk_ref[...])
    kr_ssq = ssq(kr)
    kr_rot = _rope_lanes(kr * kgr_ref[...], cos_ref[...], sa_ref[...], sb_ref[...])
    kgn = kgn_ref[...]
    for h in range(nh):
        kn_h = kn[:, h * dn:(h + 1) * dn]
        r = lax.rsqrt((ssq(kn_h) + kr_ssq) * inv_d + NORM_EPS)
        k_ref[h, :, 0:dn] = (kn_h * r * kgn).astype(BF16)
        k_ref[h, :, dn:dn + dr] = (kr_rot * r)[:, :dr].astype(BF16)


def _mla_in(x, g, wd, qlg, kvlg, wuq_t, wuk, wuv_t, qg_t, kgn, kgr, tables, tables_t):
    b, s, d = x.shape
    ts = _tile('tok', s)
    nh = MLA_HEADS
    tab = pl.BlockSpec((ts, LANES), lambda bi, i: (i, 0))
    tab_t = pl.BlockSpec((32, ts), lambda bi, i: (0, i))
    consts = [g, wd, qlg, kvlg, wuq_t, wuk, wuv_t, jnp.broadcast_to(qg_t, (MLA_QK, ts)), kgn, kgr]
    return pl.pallas_call(
        _mla_in_kernel,
        grid=(b, s // ts),
        in_specs=[pl.BlockSpec((None, ts, d), lambda bi, i: (bi, i, 0))]
                 + [_const_spec(c.shape) for c in consts] + [tab, tab, tab, tab_t, tab_t],
        out_specs=[pl.BlockSpec((None, nh, MLA_QK, ts), lambda bi, i: (bi, 0, 0, i)),
                   pl.BlockSpec((None, nh, ts, MLA_QK), lambda bi, i: (bi, 0, i, 0)),
                   pl.BlockSpec((None, nh * MLA_V, ts), lambda bi, i: (bi, 0, i))],
        out_shape=[jax.ShapeDtypeStruct((b, nh, MLA_QK, s), BF16),
                   jax.ShapeDtypeStruct((b, nh, s, MLA_QK), BF16),
                   jax.ShapeDtypeStruct((b, nh * MLA_V, s), BF16)],
        compiler_params=_params("parallel", "parallel"),
        name="mla_in",
    )(x, *consts, *tables, *tables_t)


def _prep_weights(ffn1_norm, ffn1_w_in, ffn1_w_out, mix_norm, ffn2_norm, ffn2_w_in, ffn2_w_out,
                  even_w_in, even_conv_w, even_q_norm, even_k_norm, even_lambda, even_subln, even_w_out,
                  mla_w_down, mla_q_lat_norm, mla_kv_lat_norm, mla_w_uq, mla_w_ukv, mla_q_norm, mla_k_norm, mla_w_o):
    nh, dn, dr = MLA_HEADS, MLA_NOPE, MLA_ROPE
    n_odd = mla_w_down.shape[0]
    c3 = 3 * CONV_WIDTH
    row = lambda a: a[:, None, :]
    col = lambda a: a[:, :, None]
    t_ = lambda a: jnp.swapaxes(a, 1, 2)
    wukv = mla_w_ukv.reshape(n_odd, MLA_KV_RANK, nh, dn + MLA_V)
    n_maps = DIFF_WIDTH // DIFF_HEAD_DIM
    return dict(
        ffn1_norm=row(ffn1_norm), ffn1_w_in=ffn1_w_in.astype(BF16), ffn1_w_out=ffn1_w_out.astype(BF16),
        mix_norm=row(mix_norm),
        ffn2_norm=row(ffn2_norm), ffn2_w_in=ffn2_w_in.astype(BF16), ffn2_w_out=ffn2_w_out.astype(BF16),
        even_w_tok=jnp.concatenate([even_w_in[..., :c3], even_w_in[..., c3 + DIFF_WIDTH:c3 + 2 * DIFF_WIDTH]],
                                   axis=-1).astype(BF16),
        even_w_qv=t_(jnp.concatenate([even_w_in[..., c3:c3 + DIFF_WIDTH], even_w_in[..., c3 + 2 * DIFF_WIDTH:]],
                                     axis=-1)).astype(BF16),
        even_conv_w=even_conv_w,
        even_qg_t=col(jnp.tile(even_q_norm, (1, n_maps)) * (DIFF_HEAD_DIM ** -0.5 * LOG2E)),
        even_kg=row(jnp.tile(even_k_norm, (1, n_maps))),
        even_lambda=even_lambda, even_subln=col(even_subln), even_w_out=even_w_out.astype(BF16),
        mla_w_down=jnp.pad(mla_w_down, ((0, 0), (0, 0), (0, MLA_DOWN_PAD - MLA_DOWN))).astype(BF16),
        mla_qlg=row(mla_q_lat_norm), mla_kvlg=row(mla_kv_lat_norm),
        mla_w_uq_t=t_(mla_w_uq).astype(BF16),
        mla_w_uk=wukv[..., :dn].reshape(n_odd, MLA_KV_RANK, nh * dn).astype(BF16),
        mla_w_uv_t=t_(wukv[..., dn:].reshape(n_odd, MLA_KV_RANK, nh * MLA_V)).astype(BF16),
        mla_qg_t=col(mla_q_norm * (MLA_QK ** -0.5 * LOG2E)),
        mla_kgn=row(mla_k_norm[:, :dn]),
        mla_kgr=row(jnp.pad(mla_k_norm[:, dn:], ((0, 0), (0, LANES - dr)))),
        mla_w_o=mla_w_o.astype(BF16),
        even_bound=jax.vmap(functools.partial(_score_bound, DIFF_HEAD_DIM, DIFF_HEAD_DIM ** -0.5))(
            even_q_norm, even_k_norm),
        mla_bound=jax.vmap(functools.partial(_score_bound, MLA_QK, MLA_QK ** -0.5))(mla_q_norm, mla_k_norm),
    )


def _trunk(x, p):
    b, s, d = x.shape
    depth = p['ffn1_norm'].shape[0]
    tables, tables_t = _rope_tables(s)
    flat = lambda a: a.reshape(b * s, a.shape[-1])
    x = flat(x)
    for l in range(depth):
        i = l // 2
        x = _ffn(x, p['ffn1_norm'][l], p['ffn1_w_in'][l], p['ffn1_w_out'][l])
        x3 = x.reshape(b, s, d)
        if l % 2 == 0:
            lambda_init = 0.8 - 0.6 * math.exp(-0.3 * l)
            ya, qT, k, vT = _even_in(x3, p['mix_norm'][l], p['even_w_tok'][i], p['even_w_qv'][i],
                                     p['even_conv_w'][i], p['even_qg_t'][i], p['even_kg'][i], tables, tables_t)
            yb = lax.cond(
                p['even_bound'][i] <= SCORE_BOUND,
                functools.partial(_diff_attn, lambda_init=lambda_init, bounded=True),
                functools.partial(_diff_attn, lambda_init=lambda_init, bounded=False),
                qT, k, vT, p['even_lambda'][i], p['even_subln'][i])
            mix_acts, w_mix = (flat(ya), flat(yb)), p['even_w_out'][i]
        else:
            qT, k, vT = _mla_in(x3, p['mix_norm'][l], p['mla_w_down'][i], p['mla_qlg'][i], p['mla_kvlg'][i],
                                p['mla_w_uq_t'][i], p['mla_w_uk'][i], p['mla_w_uv_t'][i], p['mla_qg_t'][i],
                                p['mla_kgn'][i], p['mla_kgr'][i], tables, tables_t)
            o = lax.cond(p['mla_bound'][i] <= SCORE_BOUND,
                         functools.partial(_mla_attn, bounded=True),
                         functools.partial(_mla_attn, bounded=False), qT, k, vT)
            mix_acts, w_mix = (flat(o),), p['mla_w_o'][i]
        x = _ffn(x, p['ffn2_norm'][l], p['ffn2_w_in'][l], p['ffn2_w_out'][l], mix_acts, w_mix)
    return x.reshape(b, s, d)


def kernel(x_prompt, x_sample, ffn1_norm, ffn1_w_in, ffn1_w_out, mix_norm, ffn2_norm, ffn2_w_in, ffn2_w_out, even_w_in, even_conv_w, even_q_norm, even_k_norm, even_lambda, even_subln, even_w_out, mla_w_down, mla_q_lat_norm, mla_kv_lat_norm, mla_w_uq, mla_w_ukv, mla_q_norm, mla_k_norm, mla_w_o):
    p = _prep_weights(ffn1_norm, ffn1_w_in, ffn1_w_out, mix_norm, ffn2_norm, ffn2_w_in, ffn2_w_out,
                      even_w_in, even_conv_w, even_q_norm, even_k_norm, even_lambda, even_subln, even_w_out,
                      mla_w_down, mla_q_lat_norm, mla_kv_lat_norm, mla_w_uq, mla_w_ukv, mla_q_norm, mla_k_norm,
                      mla_w_o)
    return (_trunk(x_prompt, p), _trunk(x_sample, p))
```

```python
import functools
import math

import jax
import jax.numpy as jnp
from jax import lax
from jax.experimental import pallas as pl
from jax.experimental.pallas import tpu as pltpu

F32 = jnp.float32
BF16 = jnp.bfloat16

NORM_EPS = 1e-6
ROPE_THETA = 10000.0
LOG2E = math.log2(math.e)
SCORE_BOUND = 60.0

D_MODEL = 1024
D_FF = 2816
CONV_WIDTH = 512
DIFF_WIDTH = 512
DIFF_HEADS = 4
DIFF_HEAD_DIM = 64
EVEN_IN = 3 * CONV_WIDTH + 3 * DIFF_WIDTH
MLA_HEADS = 8
MLA_NOPE = 128
MLA_ROPE = 64
MLA_V = 128
MLA_QK = MLA_NOPE + MLA_ROPE
MLA_Q_RANK = 384
MLA_KV_RANK = 256
MLA_DOWN = MLA_Q_RANK + MLA_KV_RANK + MLA_ROPE
MLA_DOWN_PAD = 768

LANES = 128
HALO_ROWS = 8
VMEM_LIMIT = 56 * 1024 * 1024

TILES = dict(ffn=512, tok=1024, tq=2048, tq_mla=2048, tkc=512, unroll=8)


def _tile(name, n):
    t = min(TILES[name], n)
    assert n % t == 0, (name, n, t)
    return t


def _params(*sem):
    return pltpu.CompilerParams(dimension_semantics=sem, vmem_limit_bytes=VMEM_LIMIT)


def _const_spec(shape):
    nd = len(shape)
    return pl.BlockSpec(shape, lambda *_: (0,) * nd, pipeline_mode=pl.Buffered(1))


def _rms(x, g):
    ms = jnp.mean(x * x, axis=-1, keepdims=True)
    return x * lax.rsqrt(ms + NORM_EPS) * g


def _dot(a, b):
    return jnp.dot(a, b, preferred_element_type=F32)


def _ffn_kernel(*refs, n_mix):
    x_ref, g_ref, win_ref, wout_ref = refs[:4]
    o_ref = refs[-1]
    x = x_ref[...]
    if n_mix:
        wmix_ref = refs[4 + n_mix]
        row = 0
        for a_ref in refs[4:4 + n_mix]:
            x = x + _dot(a_ref[...], wmix_ref[row:row + a_ref.shape[1], :])
            row += a_ref.shape[1]
    xn = _rms(x, g_ref[...]).astype(BF16)
    h = _dot(xn, win_ref[...])
    gate, up = h[:, :D_FF], h[:, D_FF:]
    act = (gate / (1.0 + jnp.exp(-gate)) * up).astype(BF16)
    o_ref[...] = x + 0.5 * _dot(act, wout_ref[...])


def _ffn(x, g, w_in, w_out, mix_acts=(), w_mix=None):
    t, d = x.shape
    tm = _tile('ffn', t)
    row = lambda wd: pl.BlockSpec((tm, wd), lambda i: (i, 0))
    mix_args = list(mix_acts) + ([w_mix] if mix_acts else [])
    mix_specs = [row(a.shape[1]) for a in mix_acts] + ([_const_spec(w_mix.shape)] if mix_acts else [])
    return pl.pallas_call(
        functools.partial(_ffn_kernel, n_mix=len(mix_acts)),
        grid=(t // tm,),
        in_specs=[row(d), _const_spec((1, d)), _const_spec(w_in.shape), _const_spec(w_out.shape)] + mix_specs,
        out_specs=row(d),
        out_shape=jax.ShapeDtypeStruct((t, d), F32),
        compiler_params=_params("parallel"),
        name="mix_ffn" if mix_acts else "ffn",
    )(x, g, w_in, w_out, *mix_args)


def _seg64_rsqrt(x):
    n = x.shape[0]
    lo = lax.broadcasted_iota(jnp.int32, (n, LANES), 1) < 64
    outs = []
    for p in range(x.shape[1] // LANES):
        x2 = x[:, LANES * p:LANES * (p + 1)]
        x2 = x2 * x2
        s_lo = jnp.sum(jnp.where(lo, x2, 0.0), axis=-1, keepdims=True)
        s_hi = jnp.sum(jnp.where(lo, 0.0, x2), axis=-1, keepdims=True)
        outs.append(lax.rsqrt(jnp.where(lo, s_lo, s_hi) * (1.0 / 64) + NORM_EPS))
    return jnp.concatenate(outs, axis=-1)


def _rope_lanes(x, cos, sa, sb):
    w = x.shape[1]
    rep = w // LANES
    if rep > 1:
        cos, sa, sb = (jnp.concatenate([t] * rep, axis=1) for t in (cos, sa, sb))
    return x * cos + pltpu.roll(x, w - 32, 1) * sa + pltpu.roll(x, 32, 1) * sb


def _rope_tables(s):
    d = 64
    inv = 1.0 / (ROPE_THETA ** (jnp.arange(0, d, 2, dtype=F32) / d))
    ang = jnp.arange(s, dtype=F32)[:, None] * inv[None, :]
    cos, sin = jnp.cos(ang), jnp.sin(ang)
    z = jnp.zeros_like(sin)
    cos_l = jnp.concatenate([cos, cos] * 2, axis=1)
    sa = jnp.concatenate([-sin, z] * 2, axis=1)
    sb = jnp.concatenate([z, sin] * 2, axis=1)
    return (cos_l, sa, sb), (cos.T, sin.T)


def _dot_nt(a, b):
    return lax.dot_general(a, b, (((1,), (1,)), ((), ())), preferred_element_type=F32)


def _rope_rows(x, cos_t, sin_t):
    x1, x2 = x[:32], x[32:]
    return x1 * cos_t - x2 * sin_t, x1 * sin_t + x2 * cos_t


def _even_in_kernel(x_ref, xp_ref, xn_ref, g_ref, w_ref, wqv_ref, cw_ref, qg_ref, kg_ref,
                    cos_ref, sa_ref, sb_ref, cost_ref, sint_ref, ya_ref, qT_ref, k_ref, vT_ref):
    i = pl.program_id(1)
    n = pl.num_programs(1)
    ts = x_ref.shape[0]
    c1, c2, c3 = CONV_WIDTH, 2 * CONV_WIDTH, 3 * CONV_WIDTH
    dh = DIFF_HEAD_DIM
    g = g_ref[...]
    xn = _rms(x_ref[...], g).astype(BF16)
    proj = _dot(xn, w_ref[...])
    qv = _dot_nt(wqv_ref[...], xn)
    vT_ref[...] = qv[DIFF_WIDTH:].astype(BF16)
    cos_t, sin_t = cost_ref[...], sint_ref[...]
    for j in range(DIFF_WIDTH // dh):
        blk = qv[j * dh:(j + 1) * dh]
        r = lax.rsqrt(jnp.sum(blk * blk, axis=0, keepdims=True) * (1.0 / dh) + NORM_EPS)
        o1, o2 = _rope_rows(blk * r * qg_ref[j * dh:(j + 1) * dh, :], cos_t, sin_t)
        qT_ref[j * dh:j * dh + dh // 2, :] = o1.astype(BF16)
        qT_ref[j * dh + dh // 2:(j + 1) * dh, :] = o2.astype(BF16)
    z = proj[:, c1:c2] * proj[:, c2:c3]
    xh = jnp.concatenate([xp_ref[...], xn_ref[...]], axis=0)
    ph = _dot(_rms(xh, g).astype(BF16), w_ref[:, c1:c3])
    zh = ph[:, :c1] * ph[:, c1:]
    z_before = jnp.where(i > 0, zh[HALO_ROWS - 1:HALO_ROWS], 0.0)
    z_after = jnp.where(i < n - 1, zh[HALO_ROWS:HALO_ROWS + 1], 0.0)
    row = lax.broadcasted_iota(jnp.int32, (ts, c1), 0)
    z_m1 = jnp.where(row == 0, z_before, pltpu.roll(z, 1, 0))
    z_p1 = jnp.where(row == ts - 1, z_after, pltpu.roll(z, ts - 1, 0))
    cw = cw_ref[...]
    conv = cw[0:1] * z_m1 + cw[1:2] * z + cw[2:3] * z_p1
    ya_ref[...] = (proj[:, :c1] * conv).astype(BF16)

    k = proj[:, c3:]
    k_ref[...] = _rope_lanes(k * _seg64_rsqrt(k) * kg_ref[...], cos_ref[...], sa_ref[...], sb_ref[...]).astype(BF16)


def _even_in(x, g, w_tok, w_qv, conv_w, qg_t, kg, tables, tables_t):
    b, s, d = x.shape
    ts = _tile('tok', s)
    hb = ts // HALO_ROWS
    nhb = s // HALO_ROWS
    tok = lambda w: pl.BlockSpec((None, ts, w), lambda bi, i: (bi, i, 0))
    tokT = lambda w: pl.BlockSpec((None, w, ts), lambda bi, i: (bi, 0, i))
    tab = pl.BlockSpec((ts, LANES), lambda bi, i: (i, 0))
    tab_t = pl.BlockSpec((32, ts), lambda bi, i: (0, i))
    qg_t = jnp.broadcast_to(qg_t, (DIFF_WIDTH, ts))
    return pl.pallas_call(
        _even_in_kernel,
        grid=(b, s // ts),
        in_specs=[tok(d),
                  pl.BlockSpec((None, HALO_ROWS, d), lambda bi, i: (bi, jnp.maximum(i * hb - 1, 0), 0)),
                  pl.BlockSpec((None, HALO_ROWS, d), lambda bi, i: (bi, jnp.minimum((i + 1) * hb, nhb - 1), 0)),
                  _const_spec((1, d)), _const_spec(w_tok.shape), _const_spec(w_qv.shape), _const_spec(conv_w.shape),
                  _const_spec(qg_t.shape), _const_spec(kg.shape), tab, tab, tab, tab_t, tab_t],
        out_specs=[tok(CONV_WIDTH), tokT(DIFF_WIDTH), tok(DIFF_WIDTH), tokT(DIFF_WIDTH)],
        out_shape=[jax.ShapeDtypeStruct((b, s, CONV_WIDTH), BF16),
                   jax.ShapeDtypeStruct((b, DIFF_WIDTH, s), BF16),
                   jax.ShapeDtypeStruct((b, s, DIFF_WIDTH), BF16),
                   jax.ShapeDtypeStruct((b, DIFF_WIDTH, s), BF16)],
        compiler_params=_params("parallel", "parallel"),
        name="even_in",
    )(x, x, x, g, w_tok, w_qv, conv_w, qg_t, kg, *tables, *tables_t)


def _flash_sweep(qmat, k_ref, vT_ref, tkc):
    nk = k_ref.shape[0] // tkc
    w = qmat.shape[1]
    dv = vT_ref.shape[0]

    def scores(c):
        off = pl.multiple_of(c * tkc, tkc)
        return _dot(k_ref[pl.ds(off, tkc), :], qmat)

    def consume(c, m, l, acc, s):
        off = pl.multiple_of(c * tkc, tkc)
        m_new = jnp.maximum(m, jnp.max(s, axis=0, keepdims=True))
        alpha = jnp.exp2(m - m_new)
        p = jnp.exp2(s - m_new)
        l = alpha * l + jnp.sum(p, axis=0, keepdims=True)
        acc = alpha * acc + _dot(vT_ref[:, pl.ds(off, tkc)], p.astype(BF16))
        return m_new, l, acc

    def step(c, carry):
        m, l, acc, s = carry
        s_next = scores(c + 1)
        return consume(c, m, l, acc, s) + (s_next,)

    init = (jnp.full((1, w), -jnp.inf, F32), jnp.zeros((1, w), F32), jnp.zeros((dv, w), F32), scores(0))
    m, l, acc, s = lax.fori_loop(0, nk - 1, step, init)
    _, l, acc = consume(nk - 1, m, l, acc, s)
    return l, acc


def _bounded_sweep(qmat, k_ref, vT_ref, l_ref, acc_ref, tkc):
    nk = k_ref.shape[0] // tkc
    l_ref[...] = jnp.zeros(l_ref.shape, F32)
    acc_ref[...] = jnp.zeros(acc_ref.shape, F32)

    def step(c, carry):
        off = pl.multiple_of(c * tkc, tkc)
        p = jnp.exp2(_dot(k_ref[pl.ds(off, tkc), :], qmat))
        l_ref[...] += jnp.sum(p, axis=0, keepdims=True)
        acc_ref[...] += _dot(vT_ref[:, pl.ds(off, tkc)], p.astype(BF16))
        return carry

    lax.fori_loop(0, nk, step, 0, unroll=min(nk, TILES['unroll']))
    return l_ref[...], acc_ref[...]


def _sweep(bounded, qmat, k_ref, vT_ref, l_ref, acc_ref, tkc):
    if bounded:
        return _bounded_sweep(qmat, k_ref, vT_ref, l_ref, acc_ref, tkc)
    return _flash_sweep(qmat, k_ref, vT_ref, tkc)


def _score_bound(d, scale, qg, kg):
    margin = 1.02
    return margin * scale * LOG2E * d * jnp.max(jnp.abs(qg)) * jnp.max(jnp.abs(kg))


def _diff_attn_kernel(qT_ref, k_ref, vT_ref, lam_ref, sg_ref, o_ref, qbd_ref, l_ref, acc_ref,
                      *, tkc, lambda_init, bounded):
    tq = qT_ref.shape[1]
    dh = DIFF_HEAD_DIM
    qbd_ref[...] = jnp.zeros(qbd_ref.shape, BF16)
    qbd_ref[0:dh, 0:tq] = qT_ref[0:dh, :]
    qbd_ref[dh:2 * dh, tq:2 * tq] = qT_ref[dh:2 * dh, :]
    l, acc = _sweep(bounded, qbd_ref[...], k_ref, vT_ref, l_ref, acc_ref, tkc)
    o = acc / l
    lv = lam_ref[...]
    lam = (jnp.exp(jnp.sum(lv[0:1] * lv[1:2], axis=-1, keepdims=True))
           - jnp.exp(jnp.sum(lv[2:3] * lv[3:4], axis=-1, keepdims=True)) + lambda_init)
    od = o[:, :tq] - lam * o[:, tq:]
    ms = jnp.mean(od * od, axis=0, keepdims=True)
    on = od * lax.rsqrt(ms + NORM_EPS) * sg_ref[...] * (1.0 - lambda_init)
    o_ref[...] = on.T.astype(o_ref.dtype)


def _diff_attn(qT, k, vT, lam_vecs, subln_g, lambda_init, bounded):
    b, w, s = qT.shape
    tq, tkc = _tile('tq', s), _tile('tkc', s)
    hw = 2 * DIFF_HEAD_DIM
    return pl.pallas_call(
        functools.partial(_diff_attn_kernel, tkc=tkc, lambda_init=lambda_init, bounded=bounded),
        grid=(b, DIFF_HEADS, s // tq),
        in_specs=[pl.BlockSpec((None, hw, tq), lambda bi, h, i: (bi, h, i)),
                  pl.BlockSpec((None, s, hw), lambda bi, h, i: (bi, 0, h)),
                  pl.BlockSpec((None, hw, s), lambda bi, h, i: (bi, h, 0)),
                  _const_spec(lam_vecs.shape), _const_spec(subln_g.shape)],
        out_specs=pl.BlockSpec((None, tq, hw), lambda bi, h, i: (bi, i, h)),
        out_shape=jax.ShapeDtypeStruct((b, s, w), BF16),
        scratch_shapes=[pltpu.VMEM((hw, 2 * tq), BF16), pltpu.VMEM((1, 2 * tq), F32),
                        pltpu.VMEM((hw, 2 * tq), F32)],
        compiler_params=_params("parallel", "parallel", "arbitrary"),
        name="diff_attn_bounded" if bounded else "diff_attn",
    )(qT, k, vT, lam_vecs, subln_g)


def _mla_attn_kernel(qT_ref, k_ref, vT_ref, o_ref, l_ref, acc_ref, *, tkc, bounded):
    l, acc = _sweep(bounded, qT_ref[...], k_ref, vT_ref, l_ref, acc_ref, tkc)
    o_ref[...] = (acc / l).T.astype(o_ref.dtype)


def _mla_attn(qT, k, vT, bounded):
    b, nh, dk, s = qT.shape
    tq, tkc = _tile('tq_mla', s), _tile('tkc', s)
    return pl.pallas_call(
        functools.partial(_mla_attn_kernel, tkc=tkc, bounded=bounded),
        grid=(b, nh, s // tq),
        in_specs=[pl.BlockSpec((None, None, dk, tq), lambda bi, h, i: (bi, h, 0, i)),
                  pl.BlockSpec((None, None, s, dk), lambda bi, h, i: (bi, h, 0, 0)),
                  pl.BlockSpec((None, MLA_V, s), lambda bi, h, i: (bi, h, 0))],
        out_specs=pl.BlockSpec((None, tq, MLA_V), lambda bi, h, i: (bi, i, h)),
        out_shape=jax.ShapeDtypeStruct((b, s, nh * MLA_V), BF16),
        scratch_shapes=[pltpu.VMEM((1, tq), F32), pltpu.VMEM((MLA_V, tq), F32)],
        compiler_params=_params("parallel", "parallel", "arbitrary"),
        name="mla_attn_bounded" if bounded else "mla_attn",
    )(qT, k, vT)


def _mla_in_kernel(x_ref, g_ref, wd_ref, qlg_ref, kvlg_ref, wuq_ref, wuk_ref, wuv_ref, qg_ref, kgn_ref, kgr_ref,
                   cos_ref, sa_ref, sb_ref, cost_ref, sint_ref, qT_ref, k_ref, vT_ref):
    nh, dn, dr = MLA_HEADS, MLA_NOPE, MLA_ROPE
    inv_d = 1.0 / MLA_QK
    xn = _rms(x_ref[...], g_ref[...]).astype(BF16)
    lat = _dot(xn, wd_ref[...])
    cq = _rms(lat[:, :MLA_Q_RANK], qlg_ref[...]).astype(BF16)
    ckv = _rms(lat[:, MLA_Q_RANK:MLA_Q_RANK + MLA_KV_RANK], kvlg_ref[...]).astype(BF16)
    kr = lat[:, MLA_Q_RANK + MLA_KV_RANK:]
    vT_ref[...] = _dot_nt(wuv_ref[...], ckv).astype(BF16)

    qT = _dot_nt(wuq_ref[...], cq)
    qg, cos_t, sin_t = qg_ref[...], cost_ref[...], sint_ref[...]
    for h in range(nh):
        blk = qT[h * MLA_QK:(h + 1) * MLA_QK]
        r = lax.rsqrt(jnp.sum(blk * blk, axis=0, keepdims=True) * inv_d + NORM_EPS)
        blk = blk * r * qg
        o1, o2 = _rope_rows(blk[dn:], cos_t, sin_t)
        qT_ref[h, 0:dn, :] = blk[:dn].astype(BF16)
        qT_ref[h, dn:dn + dr // 2, :] = o1.astype(BF16)
        qT_ref[h, dn + dr // 2:, :] = o2.astype(BF16)

    def ssq(a):
        return jnp.sum(a * a, axis=-1, keepdims=True)

    kn = _dot(ckv, wuk_ref[...])
    kr_ssq = ssq(kr)
    kr_rot = _rope_lanes(kr * kgr_ref[...], cos_ref[...], sa_ref[...], sb_ref[...])
    kgn = kgn_ref[...]
    for h in range(nh):
        kn_h = kn[:, h * dn:(h + 1) * dn]
        r = lax.rsqrt((ssq(kn_h) + kr_ssq) * inv_d + NORM_EPS)
        k_ref[h, :, 0:dn] = (kn_h * r * kgn).astype(BF16)
        k_ref[h, :, dn:dn + dr] = (kr_rot * r)[:, :dr].astype(BF16)


def _mla_in(x, g, wd, qlg, kvlg, wuq_t, wuk, wuv_t, qg_t, kgn, kgr, tables, tables_t):
    b, s, d = x.shape
    ts = _tile('tok', s)
    nh = MLA_HEADS
    tab = pl.BlockSpec((ts, LANES), lambda bi, i: (i, 0))
    tab_t = pl.BlockSpec((32, ts), lambda bi, i: (0, i))
    consts = [g, wd, qlg, kvlg, wuq_t, wuk, wuv_t, jnp.broadcast_to(qg_t, (MLA_QK, ts)), kgn, kgr]
    return pl.pallas_call(
        _mla_in_kernel,
        grid=(b, s // ts),
        in_specs=[pl.BlockSpec((None, ts, d), lambda bi, i: (bi, i, 0))]
                 + [_const_spec(c.shape) for c in consts] + [tab, tab, tab, tab_t, tab_t],
        out_specs=[pl.BlockSpec((None, nh, MLA_QK, ts), lambda bi, i: (bi, 0, 0, i)),
                   pl.BlockSpec((None, nh, ts, MLA_QK), lambda bi, i: (bi, 0, i, 0)),
                   pl.BlockSpec((None, nh * MLA_V, ts), lambda bi, i: (bi, 0, i))],
        out_shape=[jax.ShapeDtypeStruct((b, nh, MLA_QK, s), BF16),
                   jax.ShapeDtypeStruct((b, nh, s, MLA_QK), BF16),
                   jax.ShapeDtypeStruct((b, nh * MLA_V, s), BF16)],
        compiler_params=_params("parallel", "parallel"),
        name="mla_in",
    )(x, *consts, *tables, *tables_t)


def _prep_weights(ffn1_norm, ffn1_w_in, ffn1_w_out, mix_norm, ffn2_norm, ffn2_w_in, ffn2_w_out,
                  even_w_in, even_conv_w, even_q_norm, even_k_norm, even_lambda, even_subln, even_w_out,
                  mla_w_down, mla_q_lat_norm, mla_kv_lat_norm, mla_w_uq, mla_w_ukv, mla_q_norm, mla_k_norm, mla_w_o):
    nh, dn, dr = MLA_HEADS, MLA_NOPE, MLA_ROPE
    c3 = 3 * CONV_WIDTH
    row = lambda a: a[:, None, :]
    col = lambda a: a[:, :, None]
    n_maps = DIFF_WIDTH // DIFF_HEAD_DIM

    def layers(w, f=lambda a: a):
        return [f(w[l]).astype(BF16) for l in range(w.shape[0])]

    def split_ukv(w):
        return w.reshape(MLA_KV_RANK, nh, dn + MLA_V)

    return dict(
        ffn1_norm=row(ffn1_norm), ffn1_w_in=layers(ffn1_w_in), ffn1_w_out=layers(ffn1_w_out),
        mix_norm=row(mix_norm),
        ffn2_norm=row(ffn2_norm), ffn2_w_in=layers(ffn2_w_in), ffn2_w_out=layers(ffn2_w_out),
        even_w_tok=layers(even_w_in, lambda w: jnp.concatenate(
            [w[:, :c3], w[:, c3 + DIFF_WIDTH:c3 + 2 * DIFF_WIDTH]], axis=-1)),
        even_w_qv=layers(even_w_in, lambda w: jnp.concatenate(
            [w[:, c3:c3 + DIFF_WIDTH], w[:, c3 + 2 * DIFF_WIDTH:]], axis=-1).T),
        even_conv_w=even_conv_w,
        even_qg_t=col(jnp.tile(even_q_norm, (1, n_maps)) * (DIFF_HEAD_DIM ** -0.5 * LOG2E)),
        even_kg=row(jnp.tile(even_k_norm, (1, n_maps))),
        even_lambda=even_lambda, even_subln=col(even_subln), even_w_out=layers(even_w_out),
        mla_w_down=layers(mla_w_down, lambda w: jnp.pad(w, ((0, 0), (0, MLA_DOWN_PAD - MLA_DOWN)))),
        mla_qlg=row(mla_q_lat_norm), mla_kvlg=row(mla_kv_lat_norm),
        mla_w_uq_t=layers(mla_w_uq, lambda w: w.T),
        mla_w_uk=layers(mla_w_ukv, lambda w: split_ukv(w)[..., :dn].reshape(MLA_KV_RANK, nh * dn)),
        mla_w_uv_t=layers(mla_w_ukv, lambda w: split_ukv(w)[..., dn:].reshape(MLA_KV_RANK, nh * MLA_V).T),
        mla_qg_t=col(mla_q_norm * (MLA_QK ** -0.5 * LOG2E)),
        mla_kgn=row(mla_k_norm[:, :dn]),
        mla_kgr=row(jnp.pad(mla_k_norm[:, dn:], ((0, 0), (0, LANES - dr)))),
        mla_w_o=layers(mla_w_o),
        even_bound=jax.vmap(functools.partial(_score_bound, DIFF_HEAD_DIM, DIFF_HEAD_DIM ** -0.5))(
            even_q_norm, even_k_norm),
        mla_bound=jax.vmap(functools.partial(_score_bound, MLA_QK, MLA_QK ** -0.5))(mla_q_norm, mla_k_norm),
    )


def _trunk(x, p):
    b, s, d = x.shape
    depth = p['ffn1_norm'].shape[0]
    tables, tables_t = _rope_tables(s)
    flat = lambda a: a.reshape(b * s, a.shape[-1])
    x = flat(x)
    for l in range(depth):
        i = l // 2
        x = _ffn(x, p['ffn1_norm'][l], p['ffn1_w_in'][l], p['ffn1_w_out'][l])
        x3 = x.reshape(b, s, d)
        if l % 2 == 0:
            lambda_init = 0.8 - 0.6 * math.exp(-0.3 * l)
            ya, qT, k, vT = _even_in(x3, p['mix_norm'][l], p['even_w_tok'][i], p['even_w_qv'][i],
                                     p['even_conv_w'][i], p['even_qg_t'][i], p['even_kg'][i], tables, tables_t)
            yb = lax.cond(
                p['even_bound'][i] <= SCORE_BOUND,
                functools.partial(_diff_attn, lambda_init=lambda_init, bounded=True),
                functools.partial(_diff_attn, lambda_init=lambda_init, bounded=False),
                qT, k, vT, p['even_lambda'][i], p['even_subln'][i])
            mix_acts, w_mix = (flat(ya), flat(yb)), p['even_w_out'][i]
        else:
            qT, k, vT = _mla_in(x3, p['mix_norm'][l], p['mla_w_down'][i], p['mla_qlg'][i], p['mla_kvlg'][i],
                                p['mla_w_uq_t'][i], p['mla_w_uk'][i], p['mla_w_uv_t'][i], p['mla_qg_t'][i],
                                p['mla_kgn'][i], p['mla_kgr'][i], tables, tables_t)
            o = lax.cond(p['mla_bound'][i] <= SCORE_BOUND,
                         functools.partial(_mla_attn, bounded=True),
                         functools.partial(_mla_attn, bounded=False), qT, k, vT)
            mix_acts, w_mix = (flat(o),), p['mla_w_o'][i]
        x = _ffn(x, p['ffn2_norm'][l], p['ffn2_w_in'][l], p['ffn2_w_out'][l], mix_acts, w_mix)
    return x.reshape(b, s, d)


def kernel(x_prompt, x_sample, ffn1_norm, ffn1_w_in, ffn1_w_out, mix_norm, ffn2_norm, ffn2_w_in, ffn2_w_out, even_w_in, even_conv_w, even_q_norm, even_k_norm, even_lambda, even_subln, even_w_out, mla_w_down, mla_q_lat_norm, mla_kv_lat_norm, mla_w_uq, mla_w_ukv, mla_q_norm, mla_k_norm, mla_w_o):
    p = _prep_weights(ffn1_norm, ffn1_w_in, ffn1_w_out, mix_norm, ffn2_norm, ffn2_w_in, ffn2_w_out,
                      even_w_in, even_conv_w, even_q_norm, even_k_norm, even_lambda, even_subln, even_w_out,
                      mla_w_down, mla_q_lat_norm, mla_kv_lat_norm, mla_w_uq, mla_w_ukv, mla_q_norm, mla_k_norm,
                      mla_w_o)
    return (_trunk(x_prompt, p), _trunk(x_sample, p))
```

```python
import functools
import math

import jax
import jax.numpy as jnp
from jax import lax
from jax.experimental import pallas as pl
from jax.experimental.pallas import tpu as pltpu

F32 = jnp.float32
BF16 = jnp.bfloat16

NORM_EPS = 1e-6
ROPE_THETA = 10000.0
LOG2E = math.log2(math.e)
SCORE_BOUND = 60.0

D_MODEL = 1024
D_FF = 2816
CONV_WIDTH = 512
DIFF_WIDTH = 512
DIFF_HEADS = 4
DIFF_HEAD_DIM = 64
EVEN_IN = 3 * CONV_WIDTH + 3 * DIFF_WIDTH
MLA_HEADS = 8
MLA_NOPE = 128
MLA_ROPE = 64
MLA_V = 128
MLA_QK = MLA_NOPE + MLA_ROPE
MLA_Q_RANK = 384
MLA_KV_RANK = 256
MLA_DOWN = MLA_Q_RANK + MLA_KV_RANK + MLA_ROPE
MLA_DOWN_PAD = 768

LANES = 128
HALO_ROWS = 8
VMEM_LIMIT = 56 * 1024 * 1024

TILES = dict(ffn=512, tok=1024, tq=2048, tq_mla=4096, tkc=512, unroll=8)


def _tile(name, n):
    t = min(TILES[name], n)
    assert n % t == 0, (name, n, t)
    return t


def _params(*sem):
    return pltpu.CompilerParams(dimension_semantics=sem, vmem_limit_bytes=VMEM_LIMIT)


def _const_spec(shape):
    nd = len(shape)
    return pl.BlockSpec(shape, lambda *_: (0,) * nd, pipeline_mode=pl.Buffered(1))


def _layer_spec(stack_shape, layer):
    nd = len(stack_shape) - 1
    return pl.BlockSpec((None,) + tuple(stack_shape[1:]), lambda *_: (layer,) + (0,) * nd,
                        pipeline_mode=pl.Buffered(1))


def _rms(x, g):
    ms = jnp.mean(x * x, axis=-1, keepdims=True)
    return x * lax.rsqrt(ms + NORM_EPS) * g


def _dot(a, b):
    return jnp.dot(a, b, preferred_element_type=F32)


def _ffn_kernel(*refs, n_mix):
    x_ref, g_ref, win_ref, wout_ref = refs[:4]
    o_ref = refs[-1]
    x = x_ref[...]
    if n_mix:
        wmix_ref = refs[4 + n_mix]
        row = 0
        for a_ref in refs[4:4 + n_mix]:
            x = x + _dot(a_ref[...], wmix_ref[row:row + a_ref.shape[1], :])
            row += a_ref.shape[1]
    xn = _rms(x, g_ref[...]).astype(BF16)
    h = _dot(xn, win_ref[...])
    gate, up = h[:, :D_FF], h[:, D_FF:]
    act = (gate / (1.0 + jnp.exp(-gate)) * up).astype(BF16)
    o_ref[...] = x + 0.5 * _dot(act, wout_ref[...])


def _ffn(x, g, w_in, w_out, layer, mix_acts=(), w_mix=None):
    t, d = x.shape
    tm = _tile('ffn', t)
    row = lambda wd: pl.BlockSpec((tm, wd), lambda i: (i, 0))
    mix_args = list(mix_acts) + ([w_mix] if mix_acts else [])
    mix_specs = [row(a.shape[1]) for a in mix_acts] + ([_const_spec(w_mix.shape)] if mix_acts else [])
    return pl.pallas_call(
        functools.partial(_ffn_kernel, n_mix=len(mix_acts)),
        grid=(t // tm,),
        in_specs=[row(d), _const_spec((1, d)), _layer_spec(w_in.shape, layer), _layer_spec(w_out.shape, layer)]
                 + mix_specs,
        out_specs=row(d),
        out_shape=jax.ShapeDtypeStruct((t, d), F32),
        compiler_params=_params("parallel"),
        name="mix_ffn" if mix_acts else "ffn",
    )(x, g, w_in, w_out, *mix_args)


def _seg64_rsqrt(x):
    n = x.shape[0]
    lo = lax.broadcasted_iota(jnp.int32, (n, LANES), 1) < 64
    outs = []
    for p in range(x.shape[1] // LANES):
        x2 = x[:, LANES * p:LANES * (p + 1)]
        x2 = x2 * x2
        s_lo = jnp.sum(jnp.where(lo, x2, 0.0), axis=-1, keepdims=True)
        s_hi = jnp.sum(jnp.where(lo, 0.0, x2), axis=-1, keepdims=True)
        outs.append(lax.rsqrt(jnp.where(lo, s_lo, s_hi) * (1.0 / 64) + NORM_EPS))
    return jnp.concatenate(outs, axis=-1)


def _rope_lanes(x, cos, sa, sb):
    w = x.shape[1]
    rep = w // LANES
    if rep > 1:
        cos, sa, sb = (jnp.concatenate([t] * rep, axis=1) for t in (cos, sa, sb))
    return x * cos + pltpu.roll(x, w - 32, 1) * sa + pltpu.roll(x, 32, 1) * sb


def _rope_tables(s):
    d = 64
    inv = 1.0 / (ROPE_THETA ** (jnp.arange(0, d, 2, dtype=F32) / d))
    ang = jnp.arange(s, dtype=F32)[:, None] * inv[None, :]
    cos, sin = jnp.cos(ang), jnp.sin(ang)
    z = jnp.zeros_like(sin)
    cos_l = jnp.concatenate([cos, cos] * 2, axis=1)
    sa = jnp.concatenate([-sin, z] * 2, axis=1)
    sb = jnp.concatenate([z, sin] * 2, axis=1)
    return (cos_l, sa, sb), (cos.T, sin.T)


def _dot_nt(a, b):
    return lax.dot_general(a, b, (((1,), (1,)), ((), ())), preferred_element_type=F32)


def _rope_rows(x, cos_t, sin_t):
    x1, x2 = x[:32], x[32:]
    return x1 * cos_t - x2 * sin_t, x1 * sin_t + x2 * cos_t


def _even_in_kernel(x_ref, xp_ref, xn_ref, g_ref, w_ref, wqv_ref, cw_ref, qg_ref, kg_ref,
                    cos_ref, sa_ref, sb_ref, cost_ref, sint_ref, ya_ref, qT_ref, k_ref, vT_ref):
    i = pl.program_id(1)
    n = pl.num_programs(1)
    ts = x_ref.shape[0]
    c1, c2, c3 = CONV_WIDTH, 2 * CONV_WIDTH, 3 * CONV_WIDTH
    dh = DIFF_HEAD_DIM
    g = g_ref[...]
    xn = _rms(x_ref[...], g).astype(BF16)
    proj = _dot(xn, w_ref[...])
    qv = _dot_nt(wqv_ref[...], xn)
    vT_ref[...] = qv[DIFF_WIDTH:].astype(BF16)
    cos_t, sin_t = cost_ref[...], sint_ref[...]
    for j in range(DIFF_WIDTH // dh):
        blk = qv[j * dh:(j + 1) * dh]
        r = lax.rsqrt(jnp.sum(blk * blk, axis=0, keepdims=True) * (1.0 / dh) + NORM_EPS)
        o1, o2 = _rope_rows(blk * r * qg_ref[j * dh:(j + 1) * dh, :], cos_t, sin_t)
        qT_ref[j * dh:j * dh + dh // 2, :] = o1.astype(BF16)
        qT_ref[j * dh + dh // 2:(j + 1) * dh, :] = o2.astype(BF16)
    z = proj[:, c1:c2] * proj[:, c2:c3]
    xh = jnp.concatenate([xp_ref[...], xn_ref[...]], axis=0)
    ph = _dot(_rms(xh, g).astype(BF16), w_ref[:, c1:c3])
    zh = ph[:, :c1] * ph[:, c1:]
    z_before = jnp.where(i > 0, zh[HALO_ROWS - 1:HALO_ROWS], 0.0)
    z_after = jnp.where(i < n - 1, zh[HALO_ROWS:HALO_ROWS + 1], 0.0)
    row = lax.broadcasted_iota(jnp.int32, (ts, c1), 0)
    z_m1 = jnp.where(row == 0, z_before, pltpu.roll(z, 1, 0))
    z_p1 = jnp.where(row == ts - 1, z_after, pltpu.roll(z, ts - 1, 0))
    cw = cw_ref[...]
    conv = cw[0:1] * z_m1 + cw[1:2] * z + cw[2:3] * z_p1
    ya_ref[...] = (proj[:, :c1] * conv).astype(BF16)

    k = proj[:, c3:]
    k_ref[...] = _rope_lanes(k * _seg64_rsqrt(k) * kg_ref[...], cos_ref[...], sa_ref[...], sb_ref[...]).astype(BF16)


def _even_in(x, g, w_tok, w_qv, conv_w, qg_t, kg, tables, tables_t):
    b, s, d = x.shape
    ts = _tile('tok', s)
    hb = ts // HALO_ROWS
    nhb = s // HALO_ROWS
    tok = lambda w: pl.BlockSpec((None, ts, w), lambda bi, i: (bi, i, 0))
    tokT = lambda w: pl.BlockSpec((None, w, ts), lambda bi, i: (bi, 0, i))
    tab = pl.BlockSpec((ts, LANES), lambda bi, i: (i, 0))
    tab_t = pl.BlockSpec((32, ts), lambda bi, i: (0, i))
    qg_t = jnp.broadcast_to(qg_t, (DIFF_WIDTH, ts))
    return pl.pallas_call(
        _even_in_kernel,
        grid=(b, s // ts),
        in_specs=[tok(d),
                  pl.BlockSpec((None, HALO_ROWS, d), lambda bi, i: (bi, jnp.maximum(i * hb - 1, 0), 0)),
                  pl.BlockSpec((None, HALO_ROWS, d), lambda bi, i: (bi, jnp.minimum((i + 1) * hb, nhb - 1), 0)),
                  _const_spec((1, d)), _const_spec(w_tok.shape), _const_spec(w_qv.shape), _const_spec(conv_w.shape),
                  _const_spec(qg_t.shape), _const_spec(kg.shape), tab, tab, tab, tab_t, tab_t],
        out_specs=[tok(CONV_WIDTH), tokT(DIFF_WIDTH), tok(DIFF_WIDTH), tokT(DIFF_WIDTH)],
        out_shape=[jax.ShapeDtypeStruct((b, s, CONV_WIDTH), BF16),
                   jax.ShapeDtypeStruct((b, DIFF_WIDTH, s), BF16),
                   jax.ShapeDtypeStruct((b, s, DIFF_WIDTH), BF16),
                   jax.ShapeDtypeStruct((b, DIFF_WIDTH, s), BF16)],
        compiler_params=_params("parallel", "parallel"),
        name="even_in",
    )(x, x, x, g, w_tok, w_qv, conv_w, qg_t, kg, *tables, *tables_t)


def _flash_sweep(qmat, k_ref, vT_ref, tkc):
    nk = k_ref.shape[0] // tkc
    w = qmat.shape[1]
    dv = vT_ref.shape[0]

    def scores(c):
        off = pl.multiple_of(c * tkc, tkc)
        return _dot(k_ref[pl.ds(off, tkc), :], qmat)

    def consume(c, m, l, acc, s):
        off = pl.multiple_of(c * tkc, tkc)
        m_new = jnp.maximum(m, jnp.max(s, axis=0, keepdims=True))
        alpha = jnp.exp2(m - m_new)
        p = jnp.exp2(s - m_new)
        l = alpha * l + jnp.sum(p, axis=0, keepdims=True)
        acc = alpha * acc + _dot(vT_ref[:, pl.ds(off, tkc)], p.astype(BF16))
        return m_new, l, acc

    def step(c, carry):
        m, l, acc, s = carry
        s_next = scores(c + 1)
        return consume(c, m, l, acc, s) + (s_next,)

    init = (jnp.full((1, w), -jnp.inf, F32), jnp.zeros((1, w), F32), jnp.zeros((dv, w), F32), scores(0))
    m, l, acc, s = lax.fori_loop(0, nk - 1, step, init)
    _, l, acc = consume(nk - 1, m, l, acc, s)
    return l, acc


def _bounded_sweep(qmat, k_ref, vT_ref, l_ref, acc_ref, tkc):
    nk = k_ref.shape[0] // tkc
    l_ref[...] = jnp.zeros(l_ref.shape, F32)
    acc_ref[...] = jnp.zeros(acc_ref.shape, F32)

    def step(c, carry):
        off = pl.multiple_of(c * tkc, tkc)
        p = jnp.exp2(_dot(k_ref[pl.ds(off, tkc), :], qmat))
        l_ref[...] += jnp.sum(p, axis=0, keepdims=True)
        acc_ref[...] += _dot(vT_ref[:, pl.ds(off, tkc)], p.astype(BF16))
        return carry

    lax.fori_loop(0, nk, step, 0, unroll=min(nk, TILES['unroll']))
    return l_ref[...], acc_ref[...]


def _sweep(bounded, qmat, k_ref, vT_ref, l_ref, acc_ref, tkc):
    if bounded:
        return _bounded_sweep(qmat, k_ref, vT_ref, l_ref, acc_ref, tkc)
    return _flash_sweep(qmat, k_ref, vT_ref, tkc)


def _score_bound(d, scale, qg, kg):
    margin = 1.02
    return margin * scale * LOG2E * d * jnp.max(jnp.abs(qg)) * jnp.max(jnp.abs(kg))


def _diff_attn_kernel(qT_ref, k_ref, vT_ref, lam_ref, sg_ref, o_ref, qbd_ref, l_ref, acc_ref,
                      *, tkc, lambda_init, bounded):
    tq = qT_ref.shape[1]
    dh = DIFF_HEAD_DIM
    qbd_ref[...] = jnp.zeros(qbd_ref.shape, BF16)
    qbd_ref[0:dh, 0:tq] = qT_ref[0:dh, :]
    qbd_ref[dh:2 * dh, tq:2 * tq] = qT_ref[dh:2 * dh, :]
    l, acc = _sweep(bounded, qbd_ref[...], k_ref, vT_ref, l_ref, acc_ref, tkc)
    o = acc / l
    lv = lam_ref[...]
    lam = (jnp.exp(jnp.sum(lv[0:1] * lv[1:2], axis=-1, keepdims=True))
           - jnp.exp(jnp.sum(lv[2:3] * lv[3:4], axis=-1, keepdims=True)) + lambda_init)
    od = o[:, :tq] - lam * o[:, tq:]
    ms = jnp.mean(od * od, axis=0, keepdims=True)
    on = od * lax.rsqrt(ms + NORM_EPS) * sg_ref[...] * (1.0 - lambda_init)
    o_ref[...] = on.T.astype(o_ref.dtype)


def _diff_attn(qT, k, vT, lam_vecs, subln_g, lambda_init, bounded):
    b, w, s = qT.shape
    tq, tkc = _tile('tq', s), _tile('tkc', s)
    hw = 2 * DIFF_HEAD_DIM
    return pl.pallas_call(
        functools.partial(_diff_attn_kernel, tkc=tkc, lambda_init=lambda_init, bounded=bounded),
        grid=(b, DIFF_HEADS, s // tq),
        in_specs=[pl.BlockSpec((None, hw, tq), lambda bi, h, i: (bi, h, i)),
                  pl.BlockSpec((None, s, hw), lambda bi, h, i: (bi, 0, h)),
                  pl.BlockSpec((None, hw, s), lambda bi, h, i: (bi, h, 0)),
                  _const_spec(lam_vecs.shape), _const_spec(subln_g.shape)],
        out_specs=pl.BlockSpec((None, tq, hw), lambda bi, h, i: (bi, i, h)),
        out_shape=jax.ShapeDtypeStruct((b, s, w), BF16),
        scratch_shapes=[pltpu.VMEM((hw, 2 * tq), BF16), pltpu.VMEM((1, 2 * tq), F32),
                        pltpu.VMEM((hw, 2 * tq), F32)],
        compiler_params=_params("parallel", "parallel", "arbitrary"),
        name="diff_attn_bounded" if bounded else "diff_attn",
    )(qT, k, vT, lam_vecs, subln_g)


def _mla_attn_kernel(qT_ref, k_ref, vT_ref, o_ref, l_ref, acc_ref, *, tkc, bounded):
    l, acc = _sweep(bounded, qT_ref[...], k_ref, vT_ref, l_ref, acc_ref, tkc)
    o_ref[...] = (acc / l).T.astype(o_ref.dtype)


def _mla_attn(qT, k, vT, bounded):
    b, nh, dk, s = qT.shape
    tq, tkc = _tile('tq_mla', s), _tile('tkc', s)
    return pl.pallas_call(
        functools.partial(_mla_attn_kernel, tkc=tkc, bounded=bounded),
        grid=(b, nh, s // tq),
        in_specs=[pl.BlockSpec((None, None, dk, tq), lambda bi, h, i: (bi, h, 0, i)),
                  pl.BlockSpec((None, None, s, dk), lambda bi, h, i: (bi, h, 0, 0)),
                  pl.BlockSpec((None, MLA_V, s), lambda bi, h, i: (bi, h, 0))],
        out_specs=pl.BlockSpec((None, tq, MLA_V), lambda bi, h, i: (bi, i, h)),
        out_shape=jax.ShapeDtypeStruct((b, s, nh * MLA_V), BF16),
        scratch_shapes=[pltpu.VMEM((1, tq), F32), pltpu.VMEM((MLA_V, tq), F32)],
        compiler_params=_params("parallel", "parallel", "arbitrary"),
        name="mla_attn_bounded" if bounded else "mla_attn",
    )(qT, k, vT)


def _mla_in_kernel(x_ref, g_ref, wd_ref, qlg_ref, kvlg_ref, wuq_ref, wuk_ref, wuv_ref, qg_ref, kgn_ref, kgr_ref,
                   cos_ref, sa_ref, sb_ref, cost_ref, sint_ref, qT_ref, k_ref, vT_ref):
    nh, dn, dr = MLA_HEADS, MLA_NOPE, MLA_ROPE
    inv_d = 1.0 / MLA_QK
    xn = _rms(x_ref[...], g_ref[...]).astype(BF16)
    lat = _dot(xn, wd_ref[...])
    cq = _rms(lat[:, :MLA_Q_RANK], qlg_ref[...]).astype(BF16)
    ckv = _rms(lat[:, MLA_Q_RANK:MLA_Q_RANK + MLA_KV_RANK], kvlg_ref[...]).astype(BF16)
    kr = lat[:, MLA_Q_RANK + MLA_KV_RANK:]
    vT_ref[...] = _dot_nt(wuv_ref[...], ckv).astype(BF16)

    qT = _dot_nt(wuq_ref[...], cq)
    qg, cos_t, sin_t = qg_ref[...], cost_ref[...], sint_ref[...]
    for h in range(nh):
        blk = qT[h * MLA_QK:(h + 1) * MLA_QK]
        r = lax.rsqrt(jnp.sum(blk * blk, axis=0, keepdims=True) * inv_d + NORM_EPS)
        blk = blk * r * qg
        o1, o2 = _rope_rows(blk[dn:], cos_t, sin_t)
        qT_ref[h, 0:dn, :] = blk[:dn].astype(BF16)
        qT_ref[h, dn:dn + dr // 2, :] = o1.astype(BF16)
        qT_ref[h, dn + dr // 2:, :] = o2.astype(BF16)

    def ssq(a):
        return jnp.sum(a * a, axis=-1, keepdims=True)

    kn = _dot(ckv, wuk_ref[...])
    kr_ssq = ssq(kr)
    kr_rot = _rope_lanes(kr * kgr_ref[...], cos_ref[...], sa_ref[...], sb_ref[...])
    kgn = kgn_ref[...]
    for h in range(nh):
        kn_h = kn[:, h * dn:(h + 1) * dn]
        r = lax.rsqrt((ssq(kn_h) + kr_ssq) * inv_d + NORM_EPS)
        k_ref[h, :, 0:dn] = (kn_h * r * kgn).astype(BF16)
        k_ref[h, :, dn:dn + dr] = (kr_rot * r)[:, :dr].astype(BF16)


def _mla_in(x, g, wd, qlg, kvlg, wuq_t, wuk, wuv_t, qg_t, kgn, kgr, tables, tables_t):
    b, s, d = x.shape
    ts = _tile('tok', s)
    nh = MLA_HEADS
    tab = pl.BlockSpec((ts, LANES), lambda bi, i: (i, 0))
    tab_t = pl.BlockSpec((32, ts), lambda bi, i: (0, i))
    consts = [g, wd, qlg, kvlg, wuq_t, wuk, wuv_t, jnp.broadcast_to(qg_t, (MLA_QK, ts)), kgn, kgr]
    return pl.pallas_call(
        _mla_in_kernel,
        grid=(b, s // ts),
        in_specs=[pl.BlockSpec((None, ts, d), lambda bi, i: (bi, i, 0))]
                 + [_const_spec(c.shape) for c in consts] + [tab, tab, tab, tab_t, tab_t],
        out_specs=[pl.BlockSpec((None, nh, MLA_QK, ts), lambda bi, i: (bi, 0, 0, i)),
                   pl.BlockSpec((None, nh, ts, MLA_QK), lambda bi, i: (bi, 0, i, 0)),
                   pl.BlockSpec((None, nh * MLA_V, ts), lambda bi, i: (bi, 0, i))],
        out_shape=[jax.ShapeDtypeStruct((b, nh, MLA_QK, s), BF16),
                   jax.ShapeDtypeStruct((b, nh, s, MLA_QK), BF16),
                   jax.ShapeDtypeStruct((b, nh * MLA_V, s), BF16)],
        compiler_params=_params("parallel", "parallel"),
        name="mla_in",
    )(x, *consts, *tables, *tables_t)


def _prep_weights(ffn1_norm, ffn1_w_in, ffn1_w_out, mix_norm, ffn2_norm, ffn2_w_in, ffn2_w_out,
                  even_w_in, even_conv_w, even_q_norm, even_k_norm, even_lambda, even_subln, even_w_out,
                  mla_w_down, mla_q_lat_norm, mla_kv_lat_norm, mla_w_uq, mla_w_ukv, mla_q_norm, mla_k_norm, mla_w_o):
    nh, dn, dr = MLA_HEADS, MLA_NOPE, MLA_ROPE
    c3 = 3 * CONV_WIDTH
    row = lambda a: a[:, None, :]
    col = lambda a: a[:, :, None]
    n_maps = DIFF_WIDTH // DIFF_HEAD_DIM

    def layers(w, f=lambda a: a):
        return [f(w[l]).astype(BF16) for l in range(w.shape[0])]

    def split_ukv(w):
        return w.reshape(MLA_KV_RANK, nh, dn + MLA_V)

    return dict(
        ffn1_norm=row(ffn1_norm), ffn1_w_in=ffn1_w_in.astype(BF16), ffn1_w_out=ffn1_w_out.astype(BF16),
        mix_norm=row(mix_norm),
        ffn2_norm=row(ffn2_norm), ffn2_w_in=ffn2_w_in.astype(BF16), ffn2_w_out=ffn2_w_out.astype(BF16),
        even_w_tok=layers(even_w_in, lambda w: jnp.concatenate(
            [w[:, :c3], w[:, c3 + DIFF_WIDTH:c3 + 2 * DIFF_WIDTH]], axis=-1)),
        even_w_qv=layers(even_w_in, lambda w: jnp.concatenate(
            [w[:, c3:c3 + DIFF_WIDTH], w[:, c3 + 2 * DIFF_WIDTH:]], axis=-1).T),
        even_conv_w=even_conv_w,
        even_qg_t=col(jnp.tile(even_q_norm, (1, n_maps)) * (DIFF_HEAD_DIM ** -0.5 * LOG2E)),
        even_kg=row(jnp.tile(even_k_norm, (1, n_maps))),
        even_lambda=even_lambda, even_subln=col(even_subln), even_w_out=layers(even_w_out),
        mla_w_down=layers(mla_w_down, lambda w: jnp.pad(w, ((0, 0), (0, MLA_DOWN_PAD - MLA_DOWN)))),
        mla_qlg=row(mla_q_lat_norm), mla_kvlg=row(mla_kv_lat_norm),
        mla_w_uq_t=layers(mla_w_uq, lambda w: w.T),
        mla_w_uk=layers(mla_w_ukv, lambda w: split_ukv(w)[..., :dn].reshape(MLA_KV_RANK, nh * dn)),
        mla_w_uv_t=layers(mla_w_ukv, lambda w: split_ukv(w)[..., dn:].reshape(MLA_KV_RANK, nh * MLA_V).T),
        mla_qg_t=col(mla_q_norm * (MLA_QK ** -0.5 * LOG2E)),
        mla_kgn=row(mla_k_norm[:, :dn]),
        mla_kgr=row(jnp.pad(mla_k_norm[:, dn:], ((0, 0), (0, LANES - dr)))),
        mla_w_o=layers(mla_w_o),
        even_bound=jax.vmap(functools.partial(_score_bound, DIFF_HEAD_DIM, DIFF_HEAD_DIM ** -0.5))(
            even_q_norm, even_k_norm),
        mla_bound=jax.vmap(functools.partial(_score_bound, MLA_QK, MLA_QK ** -0.5))(mla_q_norm, mla_k_norm),
    )


def _trunk(x, p):
    b, s, d = x.shape
    depth = p['ffn1_norm'].shape[0]
    tables, tables_t = _rope_tables(s)
    flat = lambda a: a.reshape(b * s, a.shape[-1])
    x = flat(x)
    for l in range(depth):
        i = l // 2
        x = _ffn(x, p['ffn1_norm'][l], p['ffn1_w_in'], p['ffn1_w_out'], l)
        x3 = x.reshape(b, s, d)
        if l % 2 == 0:
            lambda_init = 0.8 - 0.6 * math.exp(-0.3 * l)
            ya, qT, k, vT = _even_in(x3, p['mix_norm'][l], p['even_w_tok'][i], p['even_w_qv'][i],
                                     p['even_conv_w'][i], p['even_qg_t'][i], p['even_kg'][i], tables, tables_t)
            yb = lax.cond(
                p['even_bound'][i] <= SCORE_BOUND,
                functools.partial(_diff_attn, lambda_init=lambda_init, bounded=True),
                functools.partial(_diff_attn, lambda_init=lambda_init, bounded=False),
                qT, k, vT, p['even_lambda'][i], p['even_subln'][i])
            mix_acts, w_mix = (flat(ya), flat(yb)), p['even_w_out'][i]
        else:
            qT, k, vT = _mla_in(x3, p['mix_norm'][l], p['mla_w_down'][i], p['mla_qlg'][i], p['mla_kvlg'][i],
                                p['mla_w_uq_t'][i], p['mla_w_uk'][i], p['mla_w_uv_t'][i], p['mla_qg_t'][i],
                                p['mla_kgn'][i], p['mla_kgr'][i], tables, tables_t)
            o = lax.cond(p['mla_bound'][i] <= SCORE_BOUND,
                         functools.partial(_mla_attn, bounded=True),
                         functools.partial(_mla_attn, bounded=False), qT, k, vT)
            mix_acts, w_mix = (flat(o),), p['mla_w_o'][i]
        x = _ffn(x, p['ffn2_norm'][l], p['ffn2_w_in'], p['ffn2_w_out'], l, mix_acts, w_mix)
    return x.reshape(b, s, d)


def kernel(x_prompt, x_sample, ffn1_norm, ffn1_w_in, ffn1_w_out, mix_norm, ffn2_norm, ffn2_w_in, ffn2_w_out, even_w_in, even_conv_w, even_q_norm, even_k_norm, even_lambda, even_subln, even_w_out, mla_w_down, mla_q_lat_norm, mla_kv_lat_norm, mla_w_uq, mla_w_ukv, mla_q_norm, mla_k_norm, mla_w_o):
    p = _prep_weights(ffn1_norm, ffn1_w_in, ffn1_w_out, mix_norm, ffn2_norm, ffn2_w_in, ffn2_w_out,
                      even_w_in, even_conv_w, even_q_norm, even_k_norm, even_lambda, even_subln, even_w_out,
                      mla_w_down, mla_q_lat_norm, mla_kv_lat_norm, mla_w_uq, mla_w_ukv, mla_q_norm, mla_k_norm,
                      mla_w_o)
    return (_trunk(x_prompt, p), _trunk(x_sample, p))
```

```python
import functools
import math

import jax
import jax.numpy as jnp
from jax import lax
from jax.experimental import pallas as pl
from jax.experimental.pallas import tpu as pltpu

F32 = jnp.float32
BF16 = jnp.bfloat16

NORM_EPS = 1e-6
ROPE_THETA = 10000.0
LOG2E = math.log2(math.e)
SCORE_BOUND = 60.0

D_MODEL = 1024
D_FF = 2816
CONV_WIDTH = 512
DIFF_WIDTH = 512
DIFF_HEADS = 4
DIFF_HEAD_DIM = 64
EVEN_IN = 3 * CONV_WIDTH + 3 * DIFF_WIDTH
MLA_HEADS = 8
MLA_NOPE = 128
MLA_ROPE = 64
MLA_V = 128
MLA_QK = MLA_NOPE + MLA_ROPE
MLA_Q_RANK = 384
MLA_KV_RANK = 256
MLA_DOWN = MLA_Q_RANK + MLA_KV_RANK + MLA_ROPE
MLA_DOWN_PAD = 768

LANES = 128
HALO_ROWS = 8
VMEM_LIMIT = 56 * 1024 * 1024

TILES = dict(ffn=512, tok=1024, tq=4096, tq_mla=4096, tq_general=1024, tkc=512, unroll=8)


def _tile(name, n):
    t = min(TILES[name], n)
    assert n % t == 0, (name, n, t)
    return t


def _params(*sem):
    return pltpu.CompilerParams(dimension_semantics=sem, vmem_limit_bytes=VMEM_LIMIT)


def _const_spec(shape):
    nd = len(shape)
    return pl.BlockSpec(shape, lambda *_: (0,) * nd, pipeline_mode=pl.Buffered(1))


def _layer_spec(stack_shape, layer):
    nd = len(stack_shape) - 1
    return pl.BlockSpec((None,) + tuple(stack_shape[1:]), lambda *_: (layer,) + (0,) * nd,
                        pipeline_mode=pl.Buffered(1))


def _rms(x, g):
    ms = jnp.mean(x * x, axis=-1, keepdims=True)
    return x * lax.rsqrt(ms + NORM_EPS) * g


def _dot(a, b):
    return jnp.dot(a, b, preferred_element_type=F32)


def _ffn_kernel(*refs, n_mix):
    x_ref, g_ref, win_ref, wout_ref = refs[:4]
    o_ref = refs[-1]
    x = x_ref[...]
    if n_mix:
        wmix_ref = refs[4 + n_mix]
        row = 0
        for a_ref in refs[4:4 + n_mix]:
            x = x + _dot(a_ref[...], wmix_ref[row:row + a_ref.shape[1], :])
            row += a_ref.shape[1]
    xn = _rms(x, g_ref[...]).astype(BF16)
    h = _dot(xn, win_ref[...])
    gate, up = h[:, :D_FF], h[:, D_FF:]
    act = (gate / (1.0 + jnp.exp(-gate)) * up).astype(BF16)
    o_ref[...] = x + 0.5 * _dot(act, wout_ref[...])


def _ffn(x, g, w_in, w_out, layer, mix_acts=(), w_mix=None):
    t, d = x.shape
    tm = _tile('ffn', t)
    row = lambda wd: pl.BlockSpec((tm, wd), lambda i: (i, 0))
    mix_args = list(mix_acts) + ([w_mix] if mix_acts else [])
    mix_specs = [row(a.shape[1]) for a in mix_acts] + ([_const_spec(w_mix.shape)] if mix_acts else [])
    return pl.pallas_call(
        functools.partial(_ffn_kernel, n_mix=len(mix_acts)),
        grid=(t // tm,),
        in_specs=[row(d), _const_spec((1, d)), _layer_spec(w_in.shape, layer), _layer_spec(w_out.shape, layer)]
                 + mix_specs,
        out_specs=row(d),
        out_shape=jax.ShapeDtypeStruct((t, d), F32),
        compiler_params=_params("parallel"),
        name="mix_ffn" if mix_acts else "ffn",
    )(x, g, w_in, w_out, *mix_args)


def _seg64_rsqrt(x):
    n = x.shape[0]
    lo = lax.broadcasted_iota(jnp.int32, (n, LANES), 1) < 64
    outs = []
    for p in range(x.shape[1] // LANES):
        x2 = x[:, LANES * p:LANES * (p + 1)]
        x2 = x2 * x2
        s_lo = jnp.sum(jnp.where(lo, x2, 0.0), axis=-1, keepdims=True)
        s_hi = jnp.sum(jnp.where(lo, 0.0, x2), axis=-1, keepdims=True)
        outs.append(lax.rsqrt(jnp.where(lo, s_lo, s_hi) * (1.0 / 64) + NORM_EPS))
    return jnp.concatenate(outs, axis=-1)


def _rope_lanes(x, cos, sa, sb):
    w = x.shape[1]
    rep = w // LANES
    if rep > 1:
        cos, sa, sb = (jnp.concatenate([t] * rep, axis=1) for t in (cos, sa, sb))
    return x * cos + pltpu.roll(x, w - 32, 1) * sa + pltpu.roll(x, 32, 1) * sb


def _rope_tables(s):
    d = 64
    inv = 1.0 / (ROPE_THETA ** (jnp.arange(0, d, 2, dtype=F32) / d))
    pos = jnp.arange(s, dtype=F32)
    ang = pos[:, None] * jnp.tile(inv, LANES // 32)[None, :]
    first = (jnp.arange(LANES) % d < d // 2)[None, :]
    sin = jnp.sin(ang)
    sa = jnp.where(first, -sin, 0.0)
    sb = jnp.where(first, 0.0, sin)
    ang_t = inv[:, None] * pos[None, :]
    return (jnp.cos(ang), sa, sb), (jnp.cos(ang_t), jnp.sin(ang_t))


def _dot_nt(a, b):
    return lax.dot_general(a, b, (((1,), (1,)), ((), ())), preferred_element_type=F32)


def _rope_rows(x, cos_t, sin_t):
    x1, x2 = x[:32], x[32:]
    return x1 * cos_t - x2 * sin_t, x1 * sin_t + x2 * cos_t


def _even_in_kernel(x_ref, xp_ref, xn_ref, g_ref, w_ref, wqv_ref, cw_ref, qg_ref, kg_ref,
                    cos_ref, sa_ref, sb_ref, cost_ref, sint_ref, ya_ref, qT_ref, k_ref, vT_ref):
    i = pl.program_id(1)
    n = pl.num_programs(1)
    ts = x_ref.shape[0]
    c1, c2, c3 = CONV_WIDTH, 2 * CONV_WIDTH, 3 * CONV_WIDTH
    dh = DIFF_HEAD_DIM
    g = g_ref[...]
    xn = _rms(x_ref[...], g).astype(BF16)
    proj = _dot(xn, w_ref[...])
    qv = _dot_nt(wqv_ref[...], xn)
    vT_ref[...] = qv[DIFF_WIDTH:].astype(BF16)
    cos_t, sin_t = cost_ref[...], sint_ref[...]
    for j in range(DIFF_WIDTH // dh):
        blk = qv[j * dh:(j + 1) * dh]
        r = lax.rsqrt(jnp.sum(blk * blk, axis=0, keepdims=True) * (1.0 / dh) + NORM_EPS)
        o1, o2 = _rope_rows(blk * r * qg_ref[j * dh:(j + 1) * dh, :], cos_t, sin_t)
        qT_ref[j * dh:j * dh + dh // 2, :] = o1.astype(BF16)
        qT_ref[j * dh + dh // 2:(j + 1) * dh, :] = o2.astype(BF16)
    z = proj[:, c1:c2] * proj[:, c2:c3]
    xh = jnp.concatenate([xp_ref[...], xn_ref[...]], axis=0)
    ph = _dot(_rms(xh, g).astype(BF16), w_ref[:, c1:c3])
    zh = ph[:, :c1] * ph[:, c1:]
    z_before = jnp.where(i > 0, zh[HALO_ROWS - 1:HALO_ROWS], 0.0)
    z_after = jnp.where(i < n - 1, zh[HALO_ROWS:HALO_ROWS + 1], 0.0)
    row = lax.broadcasted_iota(jnp.int32, (ts, c1), 0)
    z_m1 = jnp.where(row == 0, z_before, pltpu.roll(z, 1, 0))
    z_p1 = jnp.where(row == ts - 1, z_after, pltpu.roll(z, ts - 1, 0))
    cw = cw_ref[...]
    conv = cw[0:1] * z_m1 + cw[1:2] * z + cw[2:3] * z_p1
    ya_ref[...] = (proj[:, :c1] * conv).astype(BF16)

    k = proj[:, c3:]
    k_ref[...] = _rope_lanes(k * _seg64_rsqrt(k) * kg_ref[...], cos_ref[...], sa_ref[...], sb_ref[...]).astype(BF16)


def _even_in(x, g, w_tok, w_qv, conv_w, qg_t, kg, tables, tables_t):
    b, s, d = x.shape
    ts = _tile('tok', s)
    hb = ts // HALO_ROWS
    nhb = s // HALO_ROWS
    tok = lambda w: pl.BlockSpec((None, ts, w), lambda bi, i: (bi, i, 0))
    tokT = lambda w: pl.BlockSpec((None, w, ts), lambda bi, i: (bi, 0, i))
    tab = pl.BlockSpec((ts, LANES), lambda bi, i: (i, 0))
    tab_t = pl.BlockSpec((32, ts), lambda bi, i: (0, i))
    qg_t = jnp.broadcast_to(qg_t, (DIFF_WIDTH, ts))
    return pl.pallas_call(
        _even_in_kernel,
        grid=(b, s // ts),
        in_specs=[tok(d),
                  pl.BlockSpec((None, HALO_ROWS, d), lambda bi, i: (bi, jnp.maximum(i * hb - 1, 0), 0)),
                  pl.BlockSpec((None, HALO_ROWS, d), lambda bi, i: (bi, jnp.minimum((i + 1) * hb, nhb - 1), 0)),
                  _const_spec((1, d)), _const_spec(w_tok.shape), _const_spec(w_qv.shape), _const_spec(conv_w.shape),
                  _const_spec(qg_t.shape), _const_spec(kg.shape), tab, tab, tab, tab_t, tab_t],
        out_specs=[tok(CONV_WIDTH), tokT(DIFF_WIDTH), tok(DIFF_WIDTH), tokT(DIFF_WIDTH)],
        out_shape=[jax.ShapeDtypeStruct((b, s, CONV_WIDTH), BF16),
                   jax.ShapeDtypeStruct((b, DIFF_WIDTH, s), BF16),
                   jax.ShapeDtypeStruct((b, s, DIFF_WIDTH), BF16),
                   jax.ShapeDtypeStruct((b, DIFF_WIDTH, s), BF16)],
        compiler_params=_params("parallel", "parallel"),
        name="even_in",
    )(x, x, x, g, w_tok, w_qv, conv_w, qg_t, kg, *tables, *tables_t)


def _flash_sweep(qmat, k_ref, vT_ref, tkc):
    nk = k_ref.shape[0] // tkc
    w = qmat.shape[1]
    dv = vT_ref.shape[0]

    def scores(c):
        off = pl.multiple_of(c * tkc, tkc)
        return _dot(k_ref[pl.ds(off, tkc), :], qmat)

    def consume(c, m, l, acc, s):
        off = pl.multiple_of(c * tkc, tkc)
        m_new = jnp.maximum(m, jnp.max(s, axis=0, keepdims=True))
        alpha = jnp.exp2(m - m_new)
        p = jnp.exp2(s - m_new)
        l = alpha * l + jnp.sum(p, axis=0, keepdims=True)
        acc = alpha * acc + _dot(vT_ref[:, pl.ds(off, tkc)], p.astype(BF16))
        return m_new, l, acc

    def step(c, carry):
        m, l, acc, s = carry
        s_next = scores(c + 1)
        return consume(c, m, l, acc, s) + (s_next,)

    init = (jnp.full((1, w), -jnp.inf, F32), jnp.zeros((1, w), F32), jnp.zeros((dv, w), F32), scores(0))
    m, l, acc, s = lax.fori_loop(0, nk - 1, step, init)
    _, l, acc = consume(nk - 1, m, l, acc, s)
    return l, acc


def _bounded_sweep(qmat, k_ref, vT_ref, l_ref, acc_ref, tkc):
    nk = k_ref.shape[0] // tkc
    l_ref[...] = jnp.zeros(l_ref.shape, F32)
    acc_ref[...] = jnp.zeros(acc_ref.shape, F32)

    def step(c, carry):
        off = pl.multiple_of(c * tkc, tkc)
        p = jnp.exp2(_dot(k_ref[pl.ds(off, tkc), :], qmat))
        l_ref[...] += jnp.sum(p, axis=0, keepdims=True)
        acc_ref[...] += _dot(vT_ref[:, pl.ds(off, tkc)], p.astype(BF16))
        return carry

    lax.fori_loop(0, nk, step, 0, unroll=min(nk, TILES['unroll']))
    return l_ref[...], acc_ref[...]


def _sweep(bounded, qmat, k_ref, vT_ref, l_ref, acc_ref, tkc):
    if bounded:
        return _bounded_sweep(qmat, k_ref, vT_ref, l_ref, acc_ref, tkc)
    return _flash_sweep(qmat, k_ref, vT_ref, tkc)


def _score_bound(d, scale, qg, kg):
    margin = 1.02
    return margin * scale * LOG2E * d * jnp.max(jnp.abs(qg)) * jnp.max(jnp.abs(kg))


def _diff_attn_kernel(qT_ref, k_ref, vT_ref, lam_ref, sg_ref, o_ref, qbd_ref, l_ref, acc_ref,
                      *, tkc, lambda_init, bounded):
    tq = qT_ref.shape[1]
    dh = DIFF_HEAD_DIM
    qbd_ref[...] = jnp.zeros(qbd_ref.shape, BF16)
    qbd_ref[0:dh, 0:tq] = qT_ref[0:dh, :]
    qbd_ref[dh:2 * dh, tq:2 * tq] = qT_ref[dh:2 * dh, :]
    l, acc = _sweep(bounded, qbd_ref[...], k_ref, vT_ref, l_ref, acc_ref, tkc)
    o = acc / l
    lv = lam_ref[...]
    lam = (jnp.exp(jnp.sum(lv[0:1] * lv[1:2], axis=-1, keepdims=True))
           - jnp.exp(jnp.sum(lv[2:3] * lv[3:4], axis=-1, keepdims=True)) + lambda_init)
    od = o[:, :tq] - lam * o[:, tq:]
    ms = jnp.mean(od * od, axis=0, keepdims=True)
    on = od * lax.rsqrt(ms + NORM_EPS) * sg_ref[...] * (1.0 - lambda_init)
    o_ref[...] = on.T.astype(o_ref.dtype)


def _diff_attn(qT, k, vT, lam_vecs, subln_g, lambda_init, bounded):
    b, w, s = qT.shape
    tq, tkc = _tile('tq' if bounded else 'tq_general', s), _tile('tkc', s)
    hw = 2 * DIFF_HEAD_DIM
    return pl.pallas_call(
        functools.partial(_diff_attn_kernel, tkc=tkc, lambda_init=lambda_init, bounded=bounded),
        grid=(b, DIFF_HEADS, s // tq),
        in_specs=[pl.BlockSpec((None, hw, tq), lambda bi, h, i: (bi, h, i)),
                  pl.BlockSpec((None, s, hw), lambda bi, h, i: (bi, 0, h)),
                  pl.BlockSpec((None, hw, s), lambda bi, h, i: (bi, h, 0)),
                  _const_spec(lam_vecs.shape), _const_spec(subln_g.shape)],
        out_specs=pl.BlockSpec((None, tq, hw), lambda bi, h, i: (bi, i, h)),
        out_shape=jax.ShapeDtypeStruct((b, s, w), BF16),
        scratch_shapes=[pltpu.VMEM((hw, 2 * tq), BF16), pltpu.VMEM((1, 2 * tq), F32),
                        pltpu.VMEM((hw, 2 * tq), F32)],
        compiler_params=_params("parallel", "parallel", "arbitrary"),
        name="diff_attn_bounded" if bounded else "diff_attn",
    )(qT, k, vT, lam_vecs, subln_g)


def _mla_attn_kernel(qT_ref, k_ref, vT_ref, o_ref, l_ref, acc_ref, *, tkc, bounded):
    l, acc = _sweep(bounded, qT_ref[...], k_ref, vT_ref, l_ref, acc_ref, tkc)
    o_ref[...] = (acc / l).T.astype(o_ref.dtype)


def _mla_attn(qT, k, vT, bounded):
    b, nh, dk, s = qT.shape
    tq, tkc = _tile('tq_mla' if bounded else 'tq_general', s), _tile('tkc', s)
    return pl.pallas_call(
        functools.partial(_mla_attn_kernel, tkc=tkc, bounded=bounded),
        grid=(b, nh, s // tq),
        in_specs=[pl.BlockSpec((None, None, dk, tq), lambda bi, h, i: (bi, h, 0, i)),
                  pl.BlockSpec((None, None, s, dk), lambda bi, h, i: (bi, h, 0, 0)),
                  pl.BlockSpec((None, MLA_V, s), lambda bi, h, i: (bi, h, 0))],
        out_specs=pl.BlockSpec((None, tq, MLA_V), lambda bi, h, i: (bi, i, h)),
        out_shape=jax.ShapeDtypeStruct((b, s, nh * MLA_V), BF16),
        scratch_shapes=[pltpu.VMEM((1, tq), F32), pltpu.VMEM((MLA_V, tq), F32)],
        compiler_params=_params("parallel", "parallel", "arbitrary"),
        name="mla_attn_bounded" if bounded else "mla_attn",
    )(qT, k, vT)


def _mla_in_kernel(x_ref, g_ref, wd_ref, qlg_ref, kvlg_ref, wuq_ref, wuk_ref, wuv_ref, qg_ref, kgn_ref, kgr_ref,
                   cos_ref, sa_ref, sb_ref, cost_ref, sint_ref, qT_ref, k_ref, vT_ref):
    nh, dn, dr = MLA_HEADS, MLA_NOPE, MLA_ROPE
    inv_d = 1.0 / MLA_QK
    xn = _rms(x_ref[...], g_ref[...]).astype(BF16)
    lat = _dot(xn, wd_ref[...])
    cq = _rms(lat[:, :MLA_Q_RANK], qlg_ref[...]).astype(BF16)
    ckv = _rms(lat[:, MLA_Q_RANK:MLA_Q_RANK + MLA_KV_RANK], kvlg_ref[...]).astype(BF16)
    kr = lat[:, MLA_Q_RANK + MLA_KV_RANK:]
    vT_ref[...] = _dot_nt(wuv_ref[...], ckv).astype(BF16)

    qT = _dot_nt(wuq_ref[...], cq)
    qg, cos_t, sin_t = qg_ref[...], cost_ref[...], sint_ref[...]
    for h in range(nh):
        blk = qT[h * MLA_QK:(h + 1) * MLA_QK]
        r = lax.rsqrt(jnp.sum(blk * blk, axis=0, keepdims=True) * inv_d + NORM_EPS)
        blk = blk * r * qg
        o1, o2 = _rope_rows(blk[dn:], cos_t, sin_t)
        qT_ref[h, 0:dn, :] = blk[:dn].astype(BF16)
        qT_ref[h, dn:dn + dr // 2, :] = o1.astype(BF16)
        qT_ref[h, dn + dr // 2:, :] = o2.astype(BF16)

    def ssq(a):
        return jnp.sum(a * a, axis=-1, keepdims=True)

    kn = _dot(ckv, wuk_ref[...])
    kr_ssq = ssq(kr)
    kr_rot = _rope_lanes(kr * kgr_ref[...], cos_ref[...], sa_ref[...], sb_ref[...])
    kgn = kgn_ref[...]
    for h in range(nh):
        kn_h = kn[:, h * dn:(h + 1) * dn]
        r = lax.rsqrt((ssq(kn_h) + kr_ssq) * inv_d + NORM_EPS)
        k_ref[h, :, 0:dn] = (kn_h * r * kgn).astype(BF16)
        k_ref[h, :, dn:dn + dr] = (kr_rot * r)[:, :dr].astype(BF16)


def _mla_in(x, g, wd, qlg, kvlg, wuq_t, wuk, wuv_t, qg_t, kgn, kgr, tables, tables_t):
    b, s, d = x.shape
    ts = _tile('tok', s)
    nh = MLA_HEADS
    tab = pl.BlockSpec((ts, LANES), lambda bi, i: (i, 0))
    tab_t = pl.BlockSpec((32, ts), lambda bi, i: (0, i))
    consts = [g, wd, qlg, kvlg, wuq_t, wuk, wuv_t, jnp.broadcast_to(qg_t, (MLA_QK, ts)), kgn, kgr]
    return pl.pallas_call(
        _mla_in_kernel,
        grid=(b, s // ts),
        in_specs=[pl.BlockSpec((None, ts, d), lambda bi, i: (bi, i, 0))]
                 + [_const_spec(c.shape) for c in consts] + [tab, tab, tab, tab_t, tab_t],
        out_specs=[pl.BlockSpec((None, nh, MLA_QK, ts), lambda bi, i: (bi, 0, 0, i)),
                   pl.BlockSpec((None, nh, ts, MLA_QK), lambda bi, i: (bi, 0, i, 0)),
                   pl.BlockSpec((None, nh * MLA_V, ts), lambda bi, i: (bi, 0, i))],
        out_shape=[jax.ShapeDtypeStruct((b, nh, MLA_QK, s), BF16),
                   jax.ShapeDtypeStruct((b, nh, s, MLA_QK), BF16),
                   jax.ShapeDtypeStruct((b, nh * MLA_V, s), BF16)],
        compiler_params=_params("parallel", "parallel"),
        name="mla_in",
    )(x, *consts, *tables, *tables_t)


def _prep_weights(ffn1_norm, ffn1_w_in, ffn1_w_out, mix_norm, ffn2_norm, ffn2_w_in, ffn2_w_out,
                  even_w_in, even_conv_w, even_q_norm, even_k_norm, even_lambda, even_subln, even_w_out,
                  mla_w_down, mla_q_lat_norm, mla_kv_lat_norm, mla_w_uq, mla_w_ukv, mla_q_norm, mla_k_norm, mla_w_o):
    nh, dn, dr = MLA_HEADS, MLA_NOPE, MLA_ROPE
    c3 = 3 * CONV_WIDTH
    row = lambda a: a[:, None, :]
    col = lambda a: a[:, :, None]
    n_maps = DIFF_WIDTH // DIFF_HEAD_DIM

    def layers(w, f=lambda a: a):
        return [f(w[l]).astype(BF16) for l in range(w.shape[0])]

    def split_ukv(w):
        return w.reshape(MLA_KV_RANK, nh, dn + MLA_V)

    return dict(
        ffn1_norm=row(ffn1_norm), ffn1_w_in=ffn1_w_in.astype(BF16), ffn1_w_out=ffn1_w_out.astype(BF16),
        mix_norm=row(mix_norm),
        ffn2_norm=row(ffn2_norm), ffn2_w_in=ffn2_w_in.astype(BF16), ffn2_w_out=ffn2_w_out.astype(BF16),
        even_w_tok=layers(even_w_in, lambda w: jnp.concatenate(
            [w[:, :c3], w[:, c3 + DIFF_WIDTH:c3 + 2 * DIFF_WIDTH]], axis=-1)),
        even_w_qv=layers(even_w_in, lambda w: jnp.concatenate(
            [w[:, c3:c3 + DIFF_WIDTH], w[:, c3 + 2 * DIFF_WIDTH:]], axis=-1).T),
        even_conv_w=even_conv_w,
        even_qg_t=col(jnp.tile(even_q_norm, (1, n_maps)) * (DIFF_HEAD_DIM ** -0.5 * LOG2E)),
        even_kg=row(jnp.tile(even_k_norm, (1, n_maps))),
        even_lambda=even_lambda, even_subln=col(even_subln), even_w_out=layers(even_w_out),
        mla_w_down=layers(mla_w_down, lambda w: jnp.pad(w, ((0, 0), (0, MLA_DOWN_PAD - MLA_DOWN)))),
        mla_qlg=row(mla_q_lat_norm), mla_kvlg=row(mla_kv_lat_norm),
        mla_w_uq_t=layers(mla_w_uq, lambda w: w.T),
        mla_w_uk=layers(mla_w_ukv, lambda w: split_ukv(w)[..., :dn].reshape(MLA_KV_RANK, nh * dn)),
        mla_w_uv_t=layers(mla_w_ukv, lambda w: split_ukv(w)[..., dn:].reshape(MLA_KV_RANK, nh * MLA_V).T),
        mla_qg_t=col(mla_q_norm * (MLA_QK ** -0.5 * LOG2E)),
        mla_kgn=row(mla_k_norm[:, :dn]),
        mla_kgr=row(jnp.pad(mla_k_norm[:, dn:], ((0, 0), (0, LANES - dr)))),
        mla_w_o=layers(mla_w_o),
        even_bound=jax.vmap(functools.partial(_score_bound, DIFF_HEAD_DIM, DIFF_HEAD_DIM ** -0.5))(
            even_q_norm, even_k_norm),
        mla_bound=jax.vmap(functools.partial(_score_bound, MLA_QK, MLA_QK ** -0.5))(mla_q_norm, mla_k_norm),
    )


def _trunk(x, p):
    b, s, d = x.shape
    depth = p['ffn1_norm'].shape[0]
    tables, tables_t = _rope_tables(s)
    flat = lambda a: a.reshape(b * s, a.shape[-1])
    x = flat(x)
    for l in range(depth):
        i = l // 2
        x = _ffn(x, p['ffn1_norm'][l], p['ffn1_w_in'], p['ffn1_w_out'], l)
        x3 = x.reshape(b, s, d)
        if l % 2 == 0:
            lambda_init = 0.8 - 0.6 * math.exp(-0.3 * l)
            ya, qT, k, vT = _even_in(x3, p['mix_norm'][l], p['even_w_tok'][i], p['even_w_qv'][i],
                                     p['even_conv_w'][i], p['even_qg_t'][i], p['even_kg'][i], tables, tables_t)
            yb = lax.cond(
                p['even_bound'][i] <= SCORE_BOUND,
                functools.partial(_diff_attn, lambda_init=lambda_init, bounded=True),
                functools.partial(_diff_attn, lambda_init=lambda_init, bounded=False),
                qT, k, vT, p['even_lambda'][i], p['even_subln'][i])
            mix_acts, w_mix = (flat(ya), flat(yb)), p['even_w_out'][i]
        else:
            qT, k, vT = _mla_in(x3, p['mix_norm'][l], p['mla_w_down'][i], p['mla_qlg'][i], p['mla_kvlg'][i],
                                p['mla_w_uq_t'][i], p['mla_w_uk'][i], p['mla_w_uv_t'][i], p['mla_qg_t'][i],
                                p['mla_kgn'][i], p['mla_kgr'][i], tables, tables_t)
            o = lax.cond(p['mla_bound'][i] <= SCORE_BOUND,
                         functools.partial(_mla_attn, bounded=True),
                         functools.partial(_mla_attn, bounded=False), qT, k, vT)
            mix_acts, w_mix = (flat(o),), p['mla_w_o'][i]
        x = _ffn(x, p['ffn2_norm'][l], p['ffn2_w_in'], p['ffn2_w_out'], l, mix_acts, w_mix)
    return x.reshape(b, s, d)


def kernel(x_prompt, x_sample, ffn1_norm, ffn1_w_in, ffn1_w_out, mix_norm, ffn2_norm, ffn2_w_in, ffn2_w_out, even_w_in, even_conv_w, even_q_norm, even_k_norm, even_lambda, even_subln, even_w_out, mla_w_down, mla_q_lat_norm, mla_kv_lat_norm, mla_w_uq, mla_w_ukv, mla_q_norm, mla_k_norm, mla_w_o):
    p = _prep_weights(ffn1_norm, ffn1_w_in, ffn1_w_out, mix_norm, ffn2_norm, ffn2_w_in, ffn2_w_out,
                      even_w_in, even_conv_w, even_q_norm, even_k_norm, even_lambda, even_subln, even_w_out,
                      mla_w_down, mla_q_lat_norm, mla_kv_lat_norm, mla_w_uq, mla_w_ukv, mla_q_norm, mla_k_norm,
                      mla_w_o)
    return (_trunk(x_prompt, p), _trunk(x_sample, p))
```

```python
import functools
import math

import jax
import jax.numpy as jnp
from jax import lax
from jax.experimental import pallas as pl
from jax.experimental.pallas import tpu as pltpu

F32 = jnp.float32
BF16 = jnp.bfloat16

NORM_EPS = 1e-6
ROPE_THETA = 10000.0
LOG2E = math.log2(math.e)
SCORE_BOUND = 60.0

D_MODEL = 1024
D_FF = 2816
CONV_WIDTH = 512
DIFF_WIDTH = 512
DIFF_HEADS = 4
DIFF_HEAD_DIM = 64
EVEN_IN = 3 * CONV_WIDTH + 3 * DIFF_WIDTH
MLA_HEADS = 8
MLA_NOPE = 128
MLA_ROPE = 64
MLA_V = 128
MLA_QK = MLA_NOPE + MLA_ROPE
MLA_Q_RANK = 384
MLA_KV_RANK = 256
MLA_DOWN = MLA_Q_RANK + MLA_KV_RANK + MLA_ROPE
MLA_DOWN_PAD = 768

LANES = 128
HALO_ROWS = 8
VMEM_LIMIT = 56 * 1024 * 1024

TILES = dict(ffn=512, tok=1024, tok_sub=256, tq=4096, tq_mla=4096, tq_general=1024, tkc=512, unroll=8)


def _tile(name, n):
    t = min(TILES[name], n)
    assert n % t == 0, (name, n, t)
    return t


def _params(*sem):
    return pltpu.CompilerParams(dimension_semantics=sem, vmem_limit_bytes=VMEM_LIMIT)


def _const_spec(shape):
    nd = len(shape)
    return pl.BlockSpec(shape, lambda *_: (0,) * nd, pipeline_mode=pl.Buffered(1))


def _layer_spec(stack_shape, layer):
    nd = len(stack_shape) - 1
    return pl.BlockSpec((None,) + tuple(stack_shape[1:]), lambda *_: (layer,) + (0,) * nd,
                        pipeline_mode=pl.Buffered(1))


def _rms(x, g):
    ms = jnp.mean(x * x, axis=-1, keepdims=True)
    return x * lax.rsqrt(ms + NORM_EPS) * g


def _dot(a, b):
    return jnp.dot(a, b, preferred_element_type=F32)


def _ffn_kernel(*refs, n_mix):
    x_ref, g_ref, win_ref, wout_ref = refs[:4]
    o_ref = refs[-1]
    x = x_ref[...]
    if n_mix:
        wmix_ref = refs[4 + n_mix]
        row = 0
        for a_ref in refs[4:4 + n_mix]:
            x = x + _dot(a_ref[...], wmix_ref[row:row + a_ref.shape[1], :])
            row += a_ref.shape[1]
    xn = _rms(x, g_ref[...]).astype(BF16)
    h = _dot(xn, win_ref[...])
    gate, up = h[:, :D_FF], h[:, D_FF:]
    act = (gate / (1.0 + jnp.exp(-gate)) * up).astype(BF16)
    o_ref[...] = x + 0.5 * _dot(act, wout_ref[...])


def _ffn(x, g, w_in, w_out, layer, mix_acts=(), w_mix=None):
    t, d = x.shape
    tm = _tile('ffn', t)
    row = lambda wd: pl.BlockSpec((tm, wd), lambda i: (i, 0))
    mix_args = list(mix_acts) + ([w_mix] if mix_acts else [])
    mix_specs = [row(a.shape[1]) for a in mix_acts] + ([_const_spec(w_mix.shape)] if mix_acts else [])
    return pl.pallas_call(
        functools.partial(_ffn_kernel, n_mix=len(mix_acts)),
        grid=(t // tm,),
        in_specs=[row(d), _const_spec((1, d)), _layer_spec(w_in.shape, layer), _layer_spec(w_out.shape, layer)]
                 + mix_specs,
        out_specs=row(d),
        out_shape=jax.ShapeDtypeStruct((t, d), F32),
        compiler_params=_params("parallel"),
        name="mix_ffn" if mix_acts else "ffn",
    )(x, g, w_in, w_out, *mix_args)


def _seg64_rsqrt(x):
    n = x.shape[0]
    lo = lax.broadcasted_iota(jnp.int32, (n, LANES), 1) < 64
    outs = []
    for p in range(x.shape[1] // LANES):
        x2 = x[:, LANES * p:LANES * (p + 1)]
        x2 = x2 * x2
        s_lo = jnp.sum(jnp.where(lo, x2, 0.0), axis=-1, keepdims=True)
        s_hi = jnp.sum(jnp.where(lo, 0.0, x2), axis=-1, keepdims=True)
        outs.append(lax.rsqrt(jnp.where(lo, s_lo, s_hi) * (1.0 / 64) + NORM_EPS))
    return jnp.concatenate(outs, axis=-1)


def _rope_lanes(x, cos, sa, sb):
    w = x.shape[1]
    rep = w // LANES
    if rep > 1:
        cos, sa, sb = (jnp.concatenate([t] * rep, axis=1) for t in (cos, sa, sb))
    return x * cos + pltpu.roll(x, w - 32, 1) * sa + pltpu.roll(x, 32, 1) * sb


def _rope_tables(s):
    d = 64
    inv = 1.0 / (ROPE_THETA ** (jnp.arange(0, d, 2, dtype=F32) / d))
    pos = jnp.arange(s, dtype=F32)
    ang = pos[:, None] * jnp.tile(inv, LANES // 32)[None, :]
    first = (jnp.arange(LANES) % d < d // 2)[None, :]
    sin = jnp.sin(ang)
    sa = jnp.where(first, -sin, 0.0)
    sb = jnp.where(first, 0.0, sin)
    ang_t = inv[:, None] * pos[None, :]
    return (jnp.cos(ang), sa, sb), (jnp.cos(ang_t), jnp.sin(ang_t))


def _dot_nt(a, b):
    return lax.dot_general(a, b, (((1,), (1,)), ((), ())), preferred_element_type=F32)


def _rope_rows(x, cos_t, sin_t):
    x1, x2 = x[:32], x[32:]
    return x1 * cos_t - x2 * sin_t, x1 * sin_t + x2 * cos_t


def _even_in_kernel(x_ref, xp_ref, xn_ref, g_ref, w_ref, wqv_ref, cw_ref, qg_ref, kg_ref,
                    cos_ref, sa_ref, sb_ref, cost_ref, sint_ref, ya_ref, qT_ref, k_ref, vT_ref):
    i = pl.program_id(1)
    n = pl.num_programs(1)
    ts = x_ref.shape[0]
    c1, c2, c3 = CONV_WIDTH, 2 * CONV_WIDTH, 3 * CONV_WIDTH
    dh = DIFF_HEAD_DIM
    g = g_ref[...]
    cw = cw_ref[...]
    sub = _tile('tok_sub', ts)
    subs = [slice(r0, r0 + sub) for r0 in range(0, ts, sub)]

    def project(rows):
        xn = _rms(x_ref[rows, :], g).astype(BF16)
        return _dot(xn, w_ref[...]), _dot_nt(wqv_ref[...], xn)

    def finish(rows, proj, qv, z_before, z, z_after):
        vT_ref[:, rows] = qv[DIFF_WIDTH:].astype(BF16)
        cos_t, sin_t = cost_ref[:, rows], sint_ref[:, rows]
        for j in range(DIFF_WIDTH // dh):
            blk = qv[j * dh:(j + 1) * dh]
            r = lax.rsqrt(jnp.sum(blk * blk, axis=0, keepdims=True) * (1.0 / dh) + NORM_EPS)
            o1, o2 = _rope_rows(blk * r * qg_ref[j * dh:(j + 1) * dh, rows], cos_t, sin_t)
            qT_ref[j * dh:j * dh + dh // 2, rows] = o1.astype(BF16)
            qT_ref[j * dh + dh // 2:(j + 1) * dh, rows] = o2.astype(BF16)
        row = lax.broadcasted_iota(jnp.int32, (sub, c1), 0)
        z_m1 = jnp.where(row == 0, z_before, pltpu.roll(z, 1, 0))
        z_p1 = jnp.where(row == sub - 1, z_after, pltpu.roll(z, sub - 1, 0))
        conv = cw[0:1] * z_m1 + cw[1:2] * z + cw[2:3] * z_p1
        ya_ref[rows, :] = (proj[:, :c1] * conv).astype(BF16)
        k = proj[:, c3:]
        k_ref[rows, :] = _rope_lanes(k * _seg64_rsqrt(k) * kg_ref[...],
                                     cos_ref[rows, :], sa_ref[rows, :], sb_ref[rows, :]).astype(BF16)

    xh = jnp.concatenate([xp_ref[...], xn_ref[...]], axis=0)
    ph = _dot(_rms(xh, g).astype(BF16), w_ref[:, c1:c3])
    zh = ph[:, :c1] * ph[:, c1:]
    z_first = jnp.where(i > 0, zh[HALO_ROWS - 1:HALO_ROWS], 0.0)
    z_last = jnp.where(i < n - 1, zh[HALO_ROWS:HALO_ROWS + 1], 0.0)

    projs = [project(rows) for rows in subs]
    zs = [proj[:, c1:c2] * proj[:, c2:c3] for proj, _ in projs]
    for t, (rows, (proj, qv)) in enumerate(zip(subs, projs)):
        z_before = zs[t - 1][sub - 1:sub] if t > 0 else z_first
        z_after = zs[t + 1][0:1] if t + 1 < len(subs) else z_last
        finish(rows, proj, qv, z_before, zs[t], z_after)


def _even_in(x, g, w_tok, w_qv, conv_w, qg_t, kg, tables, tables_t):
    b, s, d = x.shape
    ts = _tile('tok', s)
    hb = ts // HALO_ROWS
    nhb = s // HALO_ROWS
    tok = lambda w: pl.BlockSpec((None, ts, w), lambda bi, i: (bi, i, 0))
    tokT = lambda w: pl.BlockSpec((None, w, ts), lambda bi, i: (bi, 0, i))
    tab = pl.BlockSpec((ts, LANES), lambda bi, i: (i, 0))
    tab_t = pl.BlockSpec((32, ts), lambda bi, i: (0, i))
    qg_t = jnp.broadcast_to(qg_t, (DIFF_WIDTH, ts))
    return pl.pallas_call(
        _even_in_kernel,
        grid=(b, s // ts),
        in_specs=[tok(d),
                  pl.BlockSpec((None, HALO_ROWS, d), lambda bi, i: (bi, jnp.maximum(i * hb - 1, 0), 0)),
                  pl.BlockSpec((None, HALO_ROWS, d), lambda bi, i: (bi, jnp.minimum((i + 1) * hb, nhb - 1), 0)),
                  _const_spec((1, d)), _const_spec(w_tok.shape), _const_spec(w_qv.shape), _const_spec(conv_w.shape),
                  _const_spec(qg_t.shape), _const_spec(kg.shape), tab, tab, tab, tab_t, tab_t],
        out_specs=[tok(CONV_WIDTH), tokT(DIFF_WIDTH), tok(DIFF_WIDTH), tokT(DIFF_WIDTH)],
        out_shape=[jax.ShapeDtypeStruct((b, s, CONV_WIDTH), BF16),
                   jax.ShapeDtypeStruct((b, DIFF_WIDTH, s), BF16),
                   jax.ShapeDtypeStruct((b, s, DIFF_WIDTH), BF16),
                   jax.ShapeDtypeStruct((b, DIFF_WIDTH, s), BF16)],
        compiler_params=_params("parallel", "parallel"),
        name="even_in",
    )(x, x, x, g, w_tok, w_qv, conv_w, qg_t, kg, *tables, *tables_t)


def _flash_sweep(qmat, k_ref, vT_ref, tkc):
    nk = k_ref.shape[0] // tkc
    w = qmat.shape[1]
    dv = vT_ref.shape[0]

    def scores(c):
        off = pl.multiple_of(c * tkc, tkc)
        return _dot(k_ref[pl.ds(off, tkc), :], qmat)

    def consume(c, m, l, acc, s):
        off = pl.multiple_of(c * tkc, tkc)
        m_new = jnp.maximum(m, jnp.max(s, axis=0, keepdims=True))
        alpha = jnp.exp2(m - m_new)
        p = jnp.exp2(s - m_new)
        l = alpha * l + jnp.sum(p, axis=0, keepdims=True)
        acc = alpha * acc + _dot(vT_ref[:, pl.ds(off, tkc)], p.astype(BF16))
        return m_new, l, acc

    def step(c, carry):
        m, l, acc, s = carry
        s_next = scores(c + 1)
        return consume(c, m, l, acc, s) + (s_next,)

    init = (jnp.full((1, w), -jnp.inf, F32), jnp.zeros((1, w), F32), jnp.zeros((dv, w), F32), scores(0))
    m, l, acc, s = lax.fori_loop(0, nk - 1, step, init)
    _, l, acc = consume(nk - 1, m, l, acc, s)
    return l, acc


def _bounded_sweep(qmat, k_ref, vT_ref, l_ref, acc_ref, tkc):
    nk = k_ref.shape[0] // tkc
    l_ref[...] = jnp.zeros(l_ref.shape, F32)
    acc_ref[...] = jnp.zeros(acc_ref.shape, F32)

    def step(c, carry):
        off = pl.multiple_of(c * tkc, tkc)
        p = jnp.exp2(_dot(k_ref[pl.ds(off, tkc), :], qmat))
        l_ref[...] += jnp.sum(p, axis=0, keepdims=True)
        acc_ref[...] += _dot(vT_ref[:, pl.ds(off, tkc)], p.astype(BF16))
        return carry

    lax.fori_loop(0, nk, step, 0, unroll=min(nk, TILES['unroll']))
    return l_ref[...], acc_ref[...]


def _sweep(bounded, qmat, k_ref, vT_ref, l_ref, acc_ref, tkc):
    if bounded:
        return _bounded_sweep(qmat, k_ref, vT_ref, l_ref, acc_ref, tkc)
    return _flash_sweep(qmat, k_ref, vT_ref, tkc)


def _score_bound(d, scale, qg, kg):
    margin = 1.02
    return margin * scale * LOG2E * d * jnp.max(jnp.abs(qg)) * jnp.max(jnp.abs(kg))


def _diff_attn_kernel(qT_ref, k_ref, vT_ref, lam_ref, sg_ref, o_ref, qbd_ref, l_ref, acc_ref,
                      *, tkc, lambda_init, bounded):
    tq = qT_ref.shape[1]
    dh = DIFF_HEAD_DIM
    qbd_ref[...] = jnp.zeros(qbd_ref.shape, BF16)
    qbd_ref[0:dh, 0:tq] = qT_ref[0:dh, :]
    qbd_ref[dh:2 * dh, tq:2 * tq] = qT_ref[dh:2 * dh, :]
    l, acc = _sweep(bounded, qbd_ref[...], k_ref, vT_ref, l_ref, acc_ref, tkc)
    o = acc / l
    lv = lam_ref[...]
    lam = (jnp.exp(jnp.sum(lv[0:1] * lv[1:2], axis=-1, keepdims=True))
           - jnp.exp(jnp.sum(lv[2:3] * lv[3:4], axis=-1, keepdims=True)) + lambda_init)
    od = o[:, :tq] - lam * o[:, tq:]
    ms = jnp.mean(od * od, axis=0, keepdims=True)
    on = od * lax.rsqrt(ms + NORM_EPS) * sg_ref[...] * (1.0 - lambda_init)
    o_ref[...] = on.T.astype(o_ref.dtype)


def _diff_attn(qT, k, vT, lam_vecs, subln_g, lambda_init, bounded):
    b, w, s = qT.shape
    tq, tkc = _tile('tq' if bounded else 'tq_general', s), _tile('tkc', s)
    hw = 2 * DIFF_HEAD_DIM
    return pl.pallas_call(
        functools.partial(_diff_attn_kernel, tkc=tkc, lambda_init=lambda_init, bounded=bounded),
        grid=(b, DIFF_HEADS, s // tq),
        in_specs=[pl.BlockSpec((None, hw, tq), lambda bi, h, i: (bi, h, i)),
                  pl.BlockSpec((None, s, hw), lambda bi, h, i: (bi, 0, h)),
                  pl.BlockSpec((None, hw, s), lambda bi, h, i: (bi, h, 0)),
                  _const_spec(lam_vecs.shape), _const_spec(subln_g.shape)],
        out_specs=pl.BlockSpec((None, tq, hw), lambda bi, h, i: (bi, i, h)),
        out_shape=jax.ShapeDtypeStruct((b, s, w), BF16),
        scratch_shapes=[pltpu.VMEM((hw, 2 * tq), BF16), pltpu.VMEM((1, 2 * tq), F32),
                        pltpu.VMEM((hw, 2 * tq), F32)],
        compiler_params=_params("parallel", "parallel", "arbitrary"),
        name="diff_attn_bounded" if bounded else "diff_attn",
    )(qT, k, vT, lam_vecs, subln_g)


def _mla_attn_kernel(qT_ref, k_ref, vT_ref, o_ref, l_ref, acc_ref, *, tkc, bounded):
    l, acc = _sweep(bounded, qT_ref[...], k_ref, vT_ref, l_ref, acc_ref, tkc)
    o_ref[...] = (acc / l).T.astype(o_ref.dtype)


def _mla_attn(qT, k, vT, bounded):
    b, nh, dk, s = qT.shape
    tq, tkc = _tile('tq_mla' if bounded else 'tq_general', s), _tile('tkc', s)
    return pl.pallas_call(
        functools.partial(_mla_attn_kernel, tkc=tkc, bounded=bounded),
        grid=(b, nh, s // tq),
        in_specs=[pl.BlockSpec((None, None, dk, tq), lambda bi, h, i: (bi, h, 0, i)),
                  pl.BlockSpec((None, None, s, dk), lambda bi, h, i: (bi, h, 0, 0)),
                  pl.BlockSpec((None, MLA_V, s), lambda bi, h, i: (bi, h, 0))],
        out_specs=pl.BlockSpec((None, tq, MLA_V), lambda bi, h, i: (bi, i, h)),
        out_shape=jax.ShapeDtypeStruct((b, s, nh * MLA_V), BF16),
        scratch_shapes=[pltpu.VMEM((1, tq), F32), pltpu.VMEM((MLA_V, tq), F32)],
        compiler_params=_params("parallel", "parallel", "arbitrary"),
        name="mla_attn_bounded" if bounded else "mla_attn",
    )(qT, k, vT)


def _mla_in_kernel(x_ref, g_ref, wd_ref, qlg_ref, kvlg_ref, wuq_ref, wuk_ref, wuv_ref, qg_ref, kgn_ref, kgr_ref,
                   cos_ref, sa_ref, sb_ref, cost_ref, sint_ref, qT_ref, k_ref, vT_ref):
    nh, dn, dr = MLA_HEADS, MLA_NOPE, MLA_ROPE
    inv_d = 1.0 / MLA_QK
    ts = x_ref.shape[0]
    sub = _tile('tok_sub', ts)
    subs = [slice(r0, r0 + sub) for r0 in range(0, ts, sub)]

    def ssq(a):
        return jnp.sum(a * a, axis=-1, keepdims=True)

    def down(rows):
        xn = _rms(x_ref[rows, :], g_ref[...]).astype(BF16)
        return _dot(xn, wd_ref[...])

    def up(lat):
        cq = _rms(lat[:, :MLA_Q_RANK], qlg_ref[...]).astype(BF16)
        ckv = _rms(lat[:, MLA_Q_RANK:MLA_Q_RANK + MLA_KV_RANK], kvlg_ref[...]).astype(BF16)
        return _dot_nt(wuq_ref[...], cq), _dot_nt(wuv_ref[...], ckv), _dot(ckv, wuk_ref[...])

    def finish(rows, lat, qT, vT, kn):
        vT_ref[:, rows] = vT.astype(BF16)
        qg, cos_t, sin_t = qg_ref[:, rows], cost_ref[:, rows], sint_ref[:, rows]
        for h in range(nh):
            blk = qT[h * MLA_QK:(h + 1) * MLA_QK]
            r = lax.rsqrt(jnp.sum(blk * blk, axis=0, keepdims=True) * inv_d + NORM_EPS)
            blk = blk * r * qg
            o1, o2 = _rope_rows(blk[dn:], cos_t, sin_t)
            qT_ref[h, 0:dn, rows] = blk[:dn].astype(BF16)
            qT_ref[h, dn:dn + dr // 2, rows] = o1.astype(BF16)
            qT_ref[h, dn + dr // 2:, rows] = o2.astype(BF16)
        kr = lat[:, MLA_Q_RANK + MLA_KV_RANK:]
        kr_ssq = ssq(kr)
        kr_rot = _rope_lanes(kr * kgr_ref[...], cos_ref[rows, :], sa_ref[rows, :], sb_ref[rows, :])
        kgn = kgn_ref[...]
        for h in range(nh):
            kn_h = kn[:, h * dn:(h + 1) * dn]
            r = lax.rsqrt((ssq(kn_h) + kr_ssq) * inv_d + NORM_EPS)
            k_ref[h, rows, 0:dn] = (kn_h * r * kgn).astype(BF16)
            k_ref[h, rows, dn:dn + dr] = (kr_rot * r)[:, :dr].astype(BF16)

    lats = [down(rows) for rows in subs]
    ups = [up(lat) for lat in lats]
    for rows, lat, (qT, vT, kn) in zip(subs, lats, ups):
        finish(rows, lat, qT, vT, kn)


def _mla_in(x, g, wd, qlg, kvlg, wuq_t, wuk, wuv_t, qg_t, kgn, kgr, tables, tables_t):
    b, s, d = x.shape
    ts = _tile('tok', s)
    nh = MLA_HEADS
    tab = pl.BlockSpec((ts, LANES), lambda bi, i: (i, 0))
    tab_t = pl.BlockSpec((32, ts), lambda bi, i: (0, i))
    consts = [g, wd, qlg, kvlg, wuq_t, wuk, wuv_t, jnp.broadcast_to(qg_t, (MLA_QK, ts)), kgn, kgr]
    return pl.pallas_call(
        _mla_in_kernel,
        grid=(b, s // ts),
        in_specs=[pl.BlockSpec((None, ts, d), lambda bi, i: (bi, i, 0))]
                 + [_const_spec(c.shape) for c in consts] + [tab, tab, tab, tab_t, tab_t],
        out_specs=[pl.BlockSpec((None, nh, MLA_QK, ts), lambda bi, i: (bi, 0, 0, i)),
                   pl.BlockSpec((None, nh, ts, MLA_QK), lambda bi, i: (bi, 0, i, 0)),
                   pl.BlockSpec((None, nh * MLA_V, ts), lambda bi, i: (bi, 0, i))],
        out_shape=[jax.ShapeDtypeStruct((b, nh, MLA_QK, s), BF16),
                   jax.ShapeDtypeStruct((b, nh, s, MLA_QK), BF16),
                   jax.ShapeDtypeStruct((b, nh * MLA_V, s), BF16)],
        compiler_params=_params("parallel", "parallel"),
        name="mla_in",
    )(x, *consts, *tables, *tables_t)


def _prep_weights(ffn1_norm, ffn1_w_in, ffn1_w_out, mix_norm, ffn2_norm, ffn2_w_in, ffn2_w_out,
                  even_w_in, even_conv_w, even_q_norm, even_k_norm, even_lambda, even_subln, even_w_out,
                  mla_w_down, mla_q_lat_norm, mla_kv_lat_norm, mla_w_uq, mla_w_ukv, mla_q_norm, mla_k_norm, mla_w_o):
    nh, dn, dr = MLA_HEADS, MLA_NOPE, MLA_ROPE
    c3 = 3 * CONV_WIDTH
    row = lambda a: a[:, None, :]
    col = lambda a: a[:, :, None]
    n_maps = DIFF_WIDTH // DIFF_HEAD_DIM

    def layers(w, f=lambda a: a):
        return [f(w[l]).astype(BF16) for l in range(w.shape[0])]

    def split_ukv(w):
        return w.reshape(MLA_KV_RANK, nh, dn + MLA_V)

    return dict(
        ffn1_norm=row(ffn1_norm), ffn1_w_in=ffn1_w_in.astype(BF16), ffn1_w_out=ffn1_w_out.astype(BF16),
        mix_norm=row(mix_norm),
        ffn2_norm=row(ffn2_norm), ffn2_w_in=ffn2_w_in.astype(BF16), ffn2_w_out=ffn2_w_out.astype(BF16),
        even_w_tok=layers(even_w_in, lambda w: jnp.concatenate(
            [w[:, :c3], w[:, c3 + DIFF_WIDTH:c3 + 2 * DIFF_WIDTH]], axis=-1)),
        even_w_qv=layers(even_w_in, lambda w: jnp.concatenate(
            [w[:, c3:c3 + DIFF_WIDTH], w[:, c3 + 2 * DIFF_WIDTH:]], axis=-1).T),
        even_conv_w=even_conv_w,
        even_qg_t=col(jnp.tile(even_q_norm, (1, n_maps)) * (DIFF_HEAD_DIM ** -0.5 * LOG2E)),
        even_kg=row(jnp.tile(even_k_norm, (1, n_maps))),
        even_lambda=even_lambda, even_subln=col(even_subln), even_w_out=layers(even_w_out),
        mla_w_down=layers(mla_w_down, lambda w: jnp.pad(w, ((0, 0), (0, MLA_DOWN_PAD - MLA_DOWN)))),
        mla_qlg=row(mla_q_lat_norm), mla_kvlg=row(mla_kv_lat_norm),
        mla_w_uq_t=layers(mla_w_uq, lambda w: w.T),
        mla_w_uk=layers(mla_w_ukv, lambda w: split_ukv(w)[..., :dn].reshape(MLA_KV_RANK, nh * dn)),
        mla_w_uv_t=layers(mla_w_ukv, lambda w: split_ukv(w)[..., dn:].reshape(MLA_KV_RANK, nh * MLA_V).T),
        mla_qg_t=col(mla_q_norm * (MLA_QK ** -0.5 * LOG2E)),
        mla_kgn=row(mla_k_norm[:, :dn]),
        mla_kgr=row(jnp.pad(mla_k_norm[:, dn:], ((0, 0), (0, LANES - dr)))),
        mla_w_o=layers(mla_w_o),
        even_bound=jax.vmap(functools.partial(_score_bound, DIFF_HEAD_DIM, DIFF_HEAD_DIM ** -0.5))(
            even_q_norm, even_k_norm),
        mla_bound=jax.vmap(functools.partial(_score_bound, MLA_QK, MLA_QK ** -0.5))(mla_q_norm, mla_k_norm),
    )


def _trunk(x, p):
    b, s, d = x.shape
    depth = p['ffn1_norm'].shape[0]
    tables, tables_t = _rope_tables(s)
    flat = lambda a: a.reshape(b * s, a.shape[-1])
    x = flat(x)
    for l in range(depth):
        i = l // 2
        x = _ffn(x, p['ffn1_norm'][l], p['ffn1_w_in'], p['ffn1_w_out'], l)
        x3 = x.reshape(b, s, d)
        if l % 2 == 0:
            lambda_init = 0.8 - 0.6 * math.exp(-0.3 * l)
            ya, qT, k, vT = _even_in(x3, p['mix_norm'][l], p['even_w_tok'][i], p['even_w_qv'][i],
                                     p['even_conv_w'][i], p['even_qg_t'][i], p['even_kg'][i], tables, tables_t)
            yb = lax.cond(
                p['even_bound'][i] <= SCORE_BOUND,
                functools.partial(_diff_attn, lambda_init=lambda_init, bounded=True),
                functools.partial(_diff_attn, lambda_init=lambda_init, bounded=False),
                qT, k, vT, p['even_lambda'][i], p['even_subln'][i])
            mix_acts, w_mix = (flat(ya), flat(yb)), p['even_w_out'][i]
        else:
            qT, k, vT = _mla_in(x3, p['mix_norm'][l], p['mla_w_down'][i], p['mla_qlg'][i], p['mla_kvlg'][i],
                                p['mla_w_uq_t'][i], p['mla_w_uk'][i], p['mla_w_uv_t'][i], p['mla_qg_t'][i],
                                p['mla_kgn'][i], p['mla_kgr'][i], tables, tables_t)
            o = lax.cond(p['mla_bound'][i] <= SCORE_BOUND,
                         functools.partial(_mla_attn, bounded=True),
                         functools.partial(_mla_attn, bounded=False), qT, k, vT)
            mix_acts, w_mix = (flat(o),), p['mla_w_o'][i]
        x = _ffn(x, p['ffn2_norm'][l], p['ffn2_w_in'], p['ffn2_w_out'], l, mix_acts, w_mix)
    return x.reshape(b, s, d)


def kernel(x_prompt, x_sample, ffn1_norm, ffn1_w_in, ffn1_w_out, mix_norm, ffn2_norm, ffn2_w_in, ffn2_w_out, even_w_in, even_conv_w, even_q_norm, even_k_norm, even_lambda, even_subln, even_w_out, mla_w_down, mla_q_lat_norm, mla_kv_lat_norm, mla_w_uq, mla_w_ukv, mla_q_norm, mla_k_norm, mla_w_o):
    p = _prep_weights(ffn1_norm, ffn1_w_in, ffn1_w_out, mix_norm, ffn2_norm, ffn2_w_in, ffn2_w_out,
                      even_w_in, even_conv_w, even_q_norm, even_k_norm, even_lambda, even_subln, even_w_out,
                      mla_w_down, mla_q_lat_norm, mla_kv_lat_norm, mla_w_uq, mla_w_ukv, mla_q_norm, mla_k_norm,
                      mla_w_o)
    return (_trunk(x_prompt, p), _trunk(x_sample, p))
```

```python
import functools
import math

import jax
import jax.numpy as jnp
from jax import lax
from jax.experimental import pallas as pl
from jax.experimental.pallas import tpu as pltpu

F32 = jnp.float32
BF16 = jnp.bfloat16

NORM_EPS = 1e-6
ROPE_THETA = 10000.0
LOG2E = math.log2(math.e)
SCORE_BOUND = 60.0

D_MODEL = 1024
D_FF = 2816
CONV_WIDTH = 512
DIFF_WIDTH = 512
DIFF_HEADS = 4
DIFF_HEAD_DIM = 64
EVEN_IN = 3 * CONV_WIDTH + 3 * DIFF_WIDTH
MLA_HEADS = 8
MLA_NOPE = 128
MLA_ROPE = 64
MLA_V = 128
MLA_QK = MLA_NOPE + MLA_ROPE
MLA_Q_RANK = 384
MLA_KV_RANK = 256
MLA_DOWN = MLA_Q_RANK + MLA_KV_RANK + MLA_ROPE
MLA_DOWN_PAD = 768

LANES = 128
HALO_ROWS = 8
VMEM_LIMIT = 56 * 1024 * 1024

TILES = dict(ffn=512, ffn_sub=256, tok=1024, tok_sub=256, tq=4096, tq_mla=4096, tq_general=1024, tkc=512, unroll=8)


def _tile(name, n):
    t = min(TILES[name], n)
    assert n % t == 0, (name, n, t)
    return t


def _params(*sem):
    return pltpu.CompilerParams(dimension_semantics=sem, vmem_limit_bytes=VMEM_LIMIT)


def _const_spec(shape):
    nd = len(shape)
    return pl.BlockSpec(shape, lambda *_: (0,) * nd, pipeline_mode=pl.Buffered(1))


def _layer_spec(stack_shape, layer):
    nd = len(stack_shape) - 1
    return pl.BlockSpec((None,) + tuple(stack_shape[1:]), lambda *_: (layer,) + (0,) * nd,
                        pipeline_mode=pl.Buffered(1))


def _rms(x, g):
    ms = jnp.mean(x * x, axis=-1, keepdims=True)
    return x * lax.rsqrt(ms + NORM_EPS) * g


def _dot(a, b):
    return jnp.dot(a, b, preferred_element_type=F32)


def _ffn_kernel(*refs, n_mix):
    x_ref, g_ref, win_ref, wout_ref = refs[:4]
    o_ref = refs[-1]
    tm = x_ref.shape[0]
    sub = _tile('ffn_sub', tm)
    subs = [slice(r0, r0 + sub) for r0 in range(0, tm, sub)]

    def mixed(rows):
        x = x_ref[rows, :]
        if n_mix:
            wmix_ref = refs[4 + n_mix]
            row = 0
            for a_ref in refs[4:4 + n_mix]:
                x = x + _dot(a_ref[rows, :], wmix_ref[row:row + a_ref.shape[1], :])
                row += a_ref.shape[1]
        return x

    xs = [mixed(rows) for rows in subs]
    hs = [_dot(_rms(x, g_ref[...]).astype(BF16), win_ref[...]) for x in xs]
    for rows, x, h in zip(subs, xs, hs):
        gate, up = h[:, :D_FF], h[:, D_FF:]
        act = (gate / (1.0 + jnp.exp(-gate)) * up).astype(BF16)
        o_ref[rows, :] = x + 0.5 * _dot(act, wout_ref[...])


def _ffn(x, g, w_in, w_out, layer, mix_acts=(), w_mix=None):
    t, d = x.shape
    tm = _tile('ffn', t)
    row = lambda wd: pl.BlockSpec((tm, wd), lambda i: (i, 0))
    mix_args = list(mix_acts) + ([w_mix] if mix_acts else [])
    mix_specs = [row(a.shape[1]) for a in mix_acts] + ([_const_spec(w_mix.shape)] if mix_acts else [])
    return pl.pallas_call(
        functools.partial(_ffn_kernel, n_mix=len(mix_acts)),
        grid=(t // tm,),
        in_specs=[row(d), _const_spec((1, d)), _layer_spec(w_in.shape, layer), _layer_spec(w_out.shape, layer)]
                 + mix_specs,
        out_specs=row(d),
        out_shape=jax.ShapeDtypeStruct((t, d), F32),
        compiler_params=_params("parallel"),
        name="mix_ffn" if mix_acts else "ffn",
    )(x, g, w_in, w_out, *mix_args)


def _seg64_rsqrt(x):
    n = x.shape[0]
    lo = lax.broadcasted_iota(jnp.int32, (n, LANES), 1) < 64
    outs = []
    for p in range(x.shape[1] // LANES):
        x2 = x[:, LANES * p:LANES * (p + 1)]
        x2 = x2 * x2
        s_lo = jnp.sum(jnp.where(lo, x2, 0.0), axis=-1, keepdims=True)
        s_hi = jnp.sum(jnp.where(lo, 0.0, x2), axis=-1, keepdims=True)
        outs.append(lax.rsqrt(jnp.where(lo, s_lo, s_hi) * (1.0 / 64) + NORM_EPS))
    return jnp.concatenate(outs, axis=-1)


def _rope_lanes(x, cos, sa, sb):
    w = x.shape[1]
    rep = w // LANES
    if rep > 1:
        cos, sa, sb = (jnp.concatenate([t] * rep, axis=1) for t in (cos, sa, sb))
    return x * cos + pltpu.roll(x, w - 32, 1) * sa + pltpu.roll(x, 32, 1) * sb


def _rope_tables(s):
    d = 64
    inv = 1.0 / (ROPE_THETA ** (jnp.arange(0, d, 2, dtype=F32) / d))
    pos = jnp.arange(s, dtype=F32)
    ang = pos[:, None] * jnp.tile(inv, LANES // 32)[None, :]
    first = (jnp.arange(LANES) % d < d // 2)[None, :]
    sin = jnp.sin(ang)
    sa = jnp.where(first, -sin, 0.0)
    sb = jnp.where(first, 0.0, sin)
    ang_t = inv[:, None] * pos[None, :]
    return (jnp.cos(ang), sa, sb), (jnp.cos(ang_t), jnp.sin(ang_t))


def _dot_nt(a, b):
    return lax.dot_general(a, b, (((1,), (1,)), ((), ())), preferred_element_type=F32)


def _rope_rows(x, cos_t, sin_t):
    x1, x2 = x[:32], x[32:]
    return x1 * cos_t - x2 * sin_t, x1 * sin_t + x2 * cos_t


def _even_in_kernel(x_ref, xp_ref, xn_ref, g_ref, w_ref, wqv_ref, cw_ref, qg_ref, kg_ref,
                    cos_ref, sa_ref, sb_ref, cost_ref, sint_ref, ya_ref, qT_ref, k_ref, vT_ref):
    i = pl.program_id(1)
    n = pl.num_programs(1)
    ts = x_ref.shape[0]
    c1, c2, c3 = CONV_WIDTH, 2 * CONV_WIDTH, 3 * CONV_WIDTH
    dh = DIFF_HEAD_DIM
    g = g_ref[...]
    cw = cw_ref[...]
    sub = _tile('tok_sub', ts)
    subs = [slice(r0, r0 + sub) for r0 in range(0, ts, sub)]

    def project(rows):
        xn = _rms(x_ref[rows, :], g).astype(BF16)
        return _dot(xn, w_ref[...]), _dot_nt(wqv_ref[...], xn)

    def finish(rows, proj, qv, z_before, z, z_after):
        vT_ref[:, rows] = qv[DIFF_WIDTH:].astype(BF16)
        cos_t, sin_t = cost_ref[:, rows], sint_ref[:, rows]
        for j in range(DIFF_WIDTH // dh):
            blk = qv[j * dh:(j + 1) * dh]
            r = lax.rsqrt(jnp.sum(blk * blk, axis=0, keepdims=True) * (1.0 / dh) + NORM_EPS)
            o1, o2 = _rope_rows(blk * r * qg_ref[j * dh:(j + 1) * dh, rows], cos_t, sin_t)
            qT_ref[j * dh:j * dh + dh // 2, rows] = o1.astype(BF16)
            qT_ref[j * dh + dh // 2:(j + 1) * dh, rows] = o2.astype(BF16)
        row = lax.broadcasted_iota(jnp.int32, (sub, c1), 0)
        z_m1 = jnp.where(row == 0, z_before, pltpu.roll(z, 1, 0))
        z_p1 = jnp.where(row == sub - 1, z_after, pltpu.roll(z, sub - 1, 0))
        conv = cw[0:1] * z_m1 + cw[1:2] * z + cw[2:3] * z_p1
        ya_ref[rows, :] = (proj[:, :c1] * conv).astype(BF16)
        k = proj[:, c3:]
        k_ref[rows, :] = _rope_lanes(k * _seg64_rsqrt(k) * kg_ref[...],
                                     cos_ref[rows, :], sa_ref[rows, :], sb_ref[rows, :]).astype(BF16)

    xh = jnp.concatenate([xp_ref[...], xn_ref[...]], axis=0)
    ph = _dot(_rms(xh, g).astype(BF16), w_ref[:, c1:c3])
    zh = ph[:, :c1] * ph[:, c1:]
    z_first = jnp.where(i > 0, zh[HALO_ROWS - 1:HALO_ROWS], 0.0)
    z_last = jnp.where(i < n - 1, zh[HALO_ROWS:HALO_ROWS + 1], 0.0)

    projs = [project(rows) for rows in subs]
    zs = [proj[:, c1:c2] * proj[:, c2:c3] for proj, _ in projs]
    for t, (rows, (proj, qv)) in enumerate(zip(subs, projs)):
        z_before = zs[t - 1][sub - 1:sub] if t > 0 else z_first
        z_after = zs[t + 1][0:1] if t + 1 < len(subs) else z_last
        finish(rows, proj, qv, z_before, zs[t], z_after)


def _even_in(x, g, w_tok, w_qv, conv_w, qg_t, kg, tables, tables_t):
    b, s, d = x.shape
    ts = _tile('tok', s)
    hb = ts // HALO_ROWS
    nhb = s // HALO_ROWS
    tok = lambda w: pl.BlockSpec((None, ts, w), lambda bi, i: (bi, i, 0))
    tokT = lambda w: pl.BlockSpec((None, w, ts), lambda bi, i: (bi, 0, i))
    tab = pl.BlockSpec((ts, LANES), lambda bi, i: (i, 0))
    tab_t = pl.BlockSpec((32, ts), lambda bi, i: (0, i))
    qg_t = jnp.broadcast_to(qg_t, (DIFF_WIDTH, ts))
    return pl.pallas_call(
        _even_in_kernel,
        grid=(b, s // ts),
        in_specs=[tok(d),
                  pl.BlockSpec((None, HALO_ROWS, d), lambda bi, i: (bi, jnp.maximum(i * hb - 1, 0), 0)),
                  pl.BlockSpec((None, HALO_ROWS, d), lambda bi, i: (bi, jnp.minimum((i + 1) * hb, nhb - 1), 0)),
                  _const_spec((1, d)), _const_spec(w_tok.shape), _const_spec(w_qv.shape), _const_spec(conv_w.shape),
                  _const_spec(qg_t.shape), _const_spec(kg.shape), tab, tab, tab, tab_t, tab_t],
        out_specs=[tok(CONV_WIDTH), tokT(DIFF_WIDTH), tok(DIFF_WIDTH), tokT(DIFF_WIDTH)],
        out_shape=[jax.ShapeDtypeStruct((b, s, CONV_WIDTH), BF16),
                   jax.ShapeDtypeStruct((b, DIFF_WIDTH, s), BF16),
                   jax.ShapeDtypeStruct((b, s, DIFF_WIDTH), BF16),
                   jax.ShapeDtypeStruct((b, DIFF_WIDTH, s), BF16)],
        compiler_params=_params("parallel", "parallel"),
        name="even_in",
    )(x, x, x, g, w_tok, w_qv, conv_w, qg_t, kg, *tables, *tables_t)


def _flash_sweep(qmat, k_ref, vT_ref, tkc):
    nk = k_ref.shape[0] // tkc
    w = qmat.shape[1]
    dv = vT_ref.shape[0]

    def scores(c):
        off = pl.multiple_of(c * tkc, tkc)
        return _dot(k_ref[pl.ds(off, tkc), :], qmat)

    def consume(c, m, l, acc, s):
        off = pl.multiple_of(c * tkc, tkc)
        m_new = jnp.maximum(m, jnp.max(s, axis=0, keepdims=True))
        alpha = jnp.exp2(m - m_new)
        p = jnp.exp2(s - m_new)
        l = alpha * l + jnp.sum(p, axis=0, keepdims=True)
        acc = alpha * acc + _dot(vT_ref[:, pl.ds(off, tkc)], p.astype(BF16))
        return m_new, l, acc

    def step(c, carry):
        m, l, acc, s = carry
        s_next = scores(c + 1)
        return consume(c, m, l, acc, s) + (s_next,)

    init = (jnp.full((1, w), -jnp.inf, F32), jnp.zeros((1, w), F32), jnp.zeros((dv, w), F32), scores(0))
    m, l, acc, s = lax.fori_loop(0, nk - 1, step, init)
    _, l, acc = consume(nk - 1, m, l, acc, s)
    return l, acc


def _bounded_sweep(qmat, k_ref, vT_ref, l_ref, acc_ref, tkc):
    nk = k_ref.shape[0] // tkc
    l_ref[...] = jnp.zeros(l_ref.shape, F32)
    acc_ref[...] = jnp.zeros(acc_ref.shape, F32)

    def step(c, carry):
        off = pl.multiple_of(c * tkc, tkc)
        p = jnp.exp2(_dot(k_ref[pl.ds(off, tkc), :], qmat))
        l_ref[...] += jnp.sum(p, axis=0, keepdims=True)
        acc_ref[...] += _dot(vT_ref[:, pl.ds(off, tkc)], p.astype(BF16))
        return carry

    lax.fori_loop(0, nk, step, 0, unroll=min(nk, TILES['unroll']))
    return l_ref[...], acc_ref[...]


def _sweep(bounded, qmat, k_ref, vT_ref, l_ref, acc_ref, tkc):
    if bounded:
        return _bounded_sweep(qmat, k_ref, vT_ref, l_ref, acc_ref, tkc)
    return _flash_sweep(qmat, k_ref, vT_ref, tkc)


def _score_bound(d, scale, qg, kg):
    margin = 1.02
    return margin * scale * LOG2E * d * jnp.max(jnp.abs(qg)) * jnp.max(jnp.abs(kg))


def _diff_attn_kernel(qT_ref, k_ref, vT_ref, lam_ref, sg_ref, o_ref, qbd_ref, l_ref, acc_ref,
                      *, tkc, lambda_init, bounded):
    tq = qT_ref.shape[1]
    dh = DIFF_HEAD_DIM
    qbd_ref[...] = jnp.zeros(qbd_ref.shape, BF16)
    qbd_ref[0:dh, 0:tq] = qT_ref[0:dh, :]
    qbd_ref[dh:2 * dh, tq:2 * tq] = qT_ref[dh:2 * dh, :]
    l, acc = _sweep(bounded, qbd_ref[...], k_ref, vT_ref, l_ref, acc_ref, tkc)
    o = acc / l
    lv = lam_ref[...]
    lam = (jnp.exp(jnp.sum(lv[0:1] * lv[1:2], axis=-1, keepdims=True))
           - jnp.exp(jnp.sum(lv[2:3] * lv[3:4], axis=-1, keepdims=True)) + lambda_init)
    od = o[:, :tq] - lam * o[:, tq:]
    ms = jnp.mean(od * od, axis=0, keepdims=True)
    on = od * lax.rsqrt(ms + NORM_EPS) * sg_ref[...] * (1.0 - lambda_init)
    o_ref[...] = on.T.astype(o_ref.dtype)


def _diff_attn(qT, k, vT, lam_vecs, subln_g, lambda_init, bounded):
    b, w, s = qT.shape
    tq, tkc = _tile('tq' if bounded else 'tq_general', s), _tile('tkc', s)
    hw = 2 * DIFF_HEAD_DIM
    return pl.pallas_call(
        functools.partial(_diff_attn_kernel, tkc=tkc, lambda_init=lambda_init, bounded=bounded),
        grid=(b, DIFF_HEADS, s // tq),
        in_specs=[pl.BlockSpec((None, hw, tq), lambda bi, h, i: (bi, h, i)),
                  pl.BlockSpec((None, s, hw), lambda bi, h, i: (bi, 0, h)),
                  pl.BlockSpec((None, hw, s), lambda bi, h, i: (bi, h, 0)),
                  _const_spec(lam_vecs.shape), _const_spec(subln_g.shape)],
        out_specs=pl.BlockSpec((None, tq, hw), lambda bi, h, i: (bi, i, h)),
        out_shape=jax.ShapeDtypeStruct((b, s, w), BF16),
        scratch_shapes=[pltpu.VMEM((hw, 2 * tq), BF16), pltpu.VMEM((1, 2 * tq), F32),
                        pltpu.VMEM((hw, 2 * tq), F32)],
        compiler_params=_params("parallel", "parallel", "arbitrary"),
        name="diff_attn_bounded" if bounded else "diff_attn",
    )(qT, k, vT, lam_vecs, subln_g)


def _mla_attn_kernel(qT_ref, k_ref, vT_ref, o_ref, l_ref, acc_ref, *, tkc, bounded):
    l, acc = _sweep(bounded, qT_ref[...], k_ref, vT_ref, l_ref, acc_ref, tkc)
    o_ref[...] = (acc / l).T.astype(o_ref.dtype)


def _mla_attn(qT, k, vT, bounded):
    b, nh, dk, s = qT.shape
    tq, tkc = _tile('tq_mla' if bounded else 'tq_general', s), _tile('tkc', s)
    return pl.pallas_call(
        functools.partial(_mla_attn_kernel, tkc=tkc, bounded=bounded),
        grid=(b, nh, s // tq),
        in_specs=[pl.BlockSpec((None, None, dk, tq), lambda bi, h, i: (bi, h, 0, i)),
                  pl.BlockSpec((None, None, s, dk), lambda bi, h, i: (bi, h, 0, 0)),
                  pl.BlockSpec((None, MLA_V, s), lambda bi, h, i: (bi, h, 0))],
        out_specs=pl.BlockSpec((None, tq, MLA_V), lambda bi, h, i: (bi, i, h)),
        out_shape=jax.ShapeDtypeStruct((b, s, nh * MLA_V), BF16),
        scratch_shapes=[pltpu.VMEM((1, tq), F32), pltpu.VMEM((MLA_V, tq), F32)],
        compiler_params=_params("parallel", "parallel", "arbitrary"),
        name="mla_attn_bounded" if bounded else "mla_attn",
    )(qT, k, vT)


def _mla_in_kernel(x_ref, g_ref, wd_ref, qlg_ref, kvlg_ref, wuq_ref, wuk_ref, wuv_ref, qg_ref, kgn_ref, kgr_ref,
                   cos_ref, sa_ref, sb_ref, cost_ref, sint_ref, qT_ref, k_ref, vT_ref):
    nh, dn, dr = MLA_HEADS, MLA_NOPE, MLA_ROPE
    inv_d = 1.0 / MLA_QK
    ts = x_ref.shape[0]
    sub = _tile('tok_sub', ts)
    subs = [slice(r0, r0 + sub) for r0 in range(0, ts, sub)]

    def ssq(a):
        return jnp.sum(a * a, axis=-1, keepdims=True)

    def down(rows):
        xn = _rms(x_ref[rows, :], g_ref[...]).astype(BF16)
        return _dot(xn, wd_ref[...])

    def up(lat):
        cq = _rms(lat[:, :MLA_Q_RANK], qlg_ref[...]).astype(BF16)
        ckv = _rms(lat[:, MLA_Q_RANK:MLA_Q_RANK + MLA_KV_RANK], kvlg_ref[...]).astype(BF16)
        return _dot_nt(wuq_ref[...], cq), _dot_nt(wuv_ref[...], ckv), _dot(ckv, wuk_ref[...])

    def finish(rows, lat, qT, vT, kn):
        vT_ref[:, rows] = vT.astype(BF16)
        qg, cos_t, sin_t = qg_ref[:, rows], cost_ref[:, rows], sint_ref[:, rows]
        for h in range(nh):
            blk = qT[h * MLA_QK:(h + 1) * MLA_QK]
            r = lax.rsqrt(jnp.sum(blk * blk, axis=0, keepdims=True) * inv_d + NORM_EPS)
            blk = blk * r * qg
            o1, o2 = _rope_rows(blk[dn:], cos_t, sin_t)
            qT_ref[h, 0:dn, rows] = blk[:dn].astype(BF16)
            qT_ref[h, dn:dn + dr // 2, rows] = o1.astype(BF16)
            qT_ref[h, dn + dr // 2:, rows] = o2.astype(BF16)
        kr = lat[:, MLA_Q_RANK + MLA_KV_RANK:]
        kr_ssq = ssq(kr)
        kr_rot = _rope_lanes(kr * kgr_ref[...], cos_ref[rows, :], sa_ref[rows, :], sb_ref[rows, :])
        kgn = kgn_ref[...]
        for h in range(nh):
            kn_h = kn[:, h * dn:(h + 1) * dn]
            r = lax.rsqrt((ssq(kn_h) + kr_ssq) * inv_d + NORM_EPS)
            k_ref[h, rows, 0:dn] = (kn_h * r * kgn).astype(BF16)
            k_ref[h, rows, dn:dn + dr] = (kr_rot * r)[:, :dr].astype(BF16)

    lats = [down(rows) for rows in subs]
    ups = [up(lat) for lat in lats]
    for rows, lat, (qT, vT, kn) in zip(subs, lats, ups):
        finish(rows, lat, qT, vT, kn)


def _mla_in(x, g, wd, qlg, kvlg, wuq_t, wuk, wuv_t, qg_t, kgn, kgr, tables, tables_t):
    b, s, d = x.shape
    ts = _tile('tok', s)
    nh = MLA_HEADS
    tab = pl.BlockSpec((ts, LANES), lambda bi, i: (i, 0))
    tab_t = pl.BlockSpec((32, ts), lambda bi, i: (0, i))
    consts = [g, wd, qlg, kvlg, wuq_t, wuk, wuv_t, jnp.broadcast_to(qg_t, (MLA_QK, ts)), kgn, kgr]
    return pl.pallas_call(
        _mla_in_kernel,
        grid=(b, s // ts),
        in_specs=[pl.BlockSpec((None, ts, d), lambda bi, i: (bi, i, 0))]
                 + [_const_spec(c.shape) for c in consts] + [tab, tab, tab, tab_t, tab_t],
        out_specs=[pl.BlockSpec((None, nh, MLA_QK, ts), lambda bi, i: (bi, 0, 0, i)),
                   pl.BlockSpec((None, nh, ts, MLA_QK), lambda bi, i: (bi, 0, i, 0)),
                   pl.BlockSpec((None, nh * MLA_V, ts), lambda bi, i: (bi, 0, i))],
        out_shape=[jax.ShapeDtypeStruct((b, nh, MLA_QK, s), BF16),
                   jax.ShapeDtypeStruct((b, nh, s, MLA_QK), BF16),
                   jax.ShapeDtypeStruct((b, nh * MLA_V, s), BF16)],
        compiler_params=_params("parallel", "parallel"),
        name="mla_in",
    )(x, *consts, *tables, *tables_t)


def _prep_weights(ffn1_norm, ffn1_w_in, ffn1_w_out, mix_norm, ffn2_norm, ffn2_w_in, ffn2_w_out,
                  even_w_in, even_conv_w, even_q_norm, even_k_norm, even_lambda, even_subln, even_w_out,
                  mla_w_down, mla_q_lat_norm, mla_kv_lat_norm, mla_w_uq, mla_w_ukv, mla_q_norm, mla_k_norm, mla_w_o):
    nh, dn, dr = MLA_HEADS, MLA_NOPE, MLA_ROPE
    c3 = 3 * CONV_WIDTH
    row = lambda a: a[:, None, :]
    col = lambda a: a[:, :, None]
    n_maps = DIFF_WIDTH // DIFF_HEAD_DIM

    def layers(w, f=lambda a: a):
        return [f(w[l]).astype(BF16) for l in range(w.shape[0])]

    def split_ukv(w):
        return w.reshape(MLA_KV_RANK, nh, dn + MLA_V)

    return dict(
        ffn1_norm=row(ffn1_norm), ffn1_w_in=ffn1_w_in.astype(BF16), ffn1_w_out=ffn1_w_out.astype(BF16),
        mix_norm=row(mix_norm),
        ffn2_norm=row(ffn2_norm), ffn2_w_in=ffn2_w_in.astype(BF16), ffn2_w_out=ffn2_w_out.astype(BF16),
        even_w_tok=layers(even_w_in, lambda w: jnp.concatenate(
            [w[:, :c3], w[:, c3 + DIFF_WIDTH:c3 + 2 * DIFF_WIDTH]], axis=-1)),
        even_w_qv=layers(even_w_in, lambda w: jnp.concatenate(
            [w[:, c3:c3 + DIFF_WIDTH], w[:, c3 + 2 * DIFF_WIDTH:]], axis=-1).T),
        even_conv_w=even_conv_w,
        even_qg_t=col(jnp.tile(even_q_norm, (1, n_maps)) * (DIFF_HEAD_DIM ** -0.5 * LOG2E)),
        even_kg=row(jnp.tile(even_k_norm, (1, n_maps))),
        even_lambda=even_lambda, even_subln=col(even_subln), even_w_out=layers(even_w_out),
        mla_w_down=layers(mla_w_down, lambda w: jnp.pad(w, ((0, 0), (0, MLA_DOWN_PAD - MLA_DOWN)))),
        mla_qlg=row(mla_q_lat_norm), mla_kvlg=row(mla_kv_lat_norm),
        mla_w_uq_t=layers(mla_w_uq, lambda w: w.T),
        mla_w_uk=layers(mla_w_ukv, lambda w: split_ukv(w)[..., :dn].reshape(MLA_KV_RANK, nh * dn)),
        mla_w_uv_t=layers(mla_w_ukv, lambda w: split_ukv(w)[..., dn:].reshape(MLA_KV_RANK, nh * MLA_V).T),
        mla_qg_t=col(mla_q_norm * (MLA_QK ** -0.5 * LOG2E)),
        mla_kgn=row(mla_k_norm[:, :dn]),
        mla_kgr=row(jnp.pad(mla_k_norm[:, dn:], ((0, 0), (0, LANES - dr)))),
        mla_w_o=layers(mla_w_o),
        even_bound=jax.vmap(functools.partial(_score_bound, DIFF_HEAD_DIM, DIFF_HEAD_DIM ** -0.5))(
            even_q_norm, even_k_norm),
        mla_bound=jax.vmap(functools.partial(_score_bound, MLA_QK, MLA_QK ** -0.5))(mla_q_norm, mla_k_norm),
    )


def _trunk(x, p):
    b, s, d = x.shape
    depth = p['ffn1_norm'].shape[0]
    tables, tables_t = _rope_tables(s)
    flat = lambda a: a.reshape(b * s, a.shape[-1])
    x = flat(x)
    for l in range(depth):
        i = l // 2
        x = _ffn(x, p['ffn1_norm'][l], p['ffn1_w_in'], p['ffn1_w_out'], l)
        x3 = x.reshape(b, s, d)
        if l % 2 == 0:
            lambda_init = 0.8 - 0.6 * math.exp(-0.3 * l)
            ya, qT, k, vT = _even_in(x3, p['mix_norm'][l], p['even_w_tok'][i], p['even_w_qv'][i],
                                     p['even_conv_w'][i], p['even_qg_t'][i], p['even_kg'][i], tables, tables_t)
            yb = lax.cond(
                p['even_bound'][i] <= SCORE_BOUND,
                functools.partial(_diff_attn, lambda_init=lambda_init, bounded=True),
                functools.partial(_diff_attn, lambda_init=lambda_init, bounded=False),
                qT, k, vT, p['even_lambda'][i], p['even_subln'][i])
            mix_acts, w_mix = (flat(ya), flat(yb)), p['even_w_out'][i]
        else:
            qT, k, vT = _mla_in(x3, p['mix_norm'][l], p['mla_w_down'][i], p['mla_qlg'][i], p['mla_kvlg'][i],
                                p['mla_w_uq_t'][i], p['mla_w_uk'][i], p['mla_w_uv_t'][i], p['mla_qg_t'][i],
                                p['mla_kgn'][i], p['mla_kgr'][i], tables, tables_t)
            o = lax.cond(p['mla_bound'][i] <= SCORE_BOUND,
                         functools.partial(_mla_attn, bounded=True),
                         functools.partial(_mla_attn, bounded=False), qT, k, vT)
            mix_acts, w_mix = (flat(o),), p['mla_w_o'][i]
        x = _ffn(x, p['ffn2_norm'][l], p['ffn2_w_in'], p['ffn2_w_out'], l, mix_acts, w_mix)
    return x.reshape(b, s, d)


def kernel(x_prompt, x_sample, ffn1_norm, ffn1_w_in, ffn1_w_out, mix_norm, ffn2_norm, ffn2_w_in, ffn2_w_out, even_w_in, even_conv_w, even_q_norm, even_k_norm, even_lambda, even_subln, even_w_out, mla_w_down, mla_q_lat_norm, mla_kv_lat_norm, mla_w_uq, mla_w_ukv, mla_q_norm, mla_k_norm, mla_w_o):
    p = _prep_weights(ffn1_norm, ffn1_w_in, ffn1_w_out, mix_norm, ffn2_norm, ffn2_w_in, ffn2_w_out,
                      even_w_in, even_conv_w, even_q_norm, even_k_norm, even_lambda, even_subln, even_w_out,
                      mla_w_down, mla_q_lat_norm, mla_kv_lat_norm, mla_w_uq, mla_w_ukv, mla_q_norm, mla_k_norm,
                      mla_w_o)
    return (_trunk(x_prompt, p), _trunk(x_sample, p))
```

```python
import functools
import math

import jax
import jax.numpy as jnp
from jax import lax
from jax.experimental import pallas as pl
from jax.experimental.pallas import tpu as pltpu

F32 = jnp.float32
BF16 = jnp.bfloat16

NORM_EPS = 1e-6
ROPE_THETA = 10000.0
LOG2E = math.log2(math.e)
SCORE_BOUND = 60.0

D_FF = 2816
CONV_WIDTH = 512
DIFF_WIDTH = 512
DIFF_HEADS = 4
DIFF_HEAD_DIM = 64
MLA_HEADS = 8
MLA_NOPE = 128
MLA_ROPE = 64
ROPE_DIM = 64
ROPE_HALF = ROPE_DIM // 2
assert DIFF_HEAD_DIM == MLA_ROPE == ROPE_DIM
MLA_V = 128
MLA_QK = MLA_NOPE + MLA_ROPE
MLA_Q_RANK = 384
MLA_KV_RANK = 256
MLA_DOWN = MLA_Q_RANK + MLA_KV_RANK + MLA_ROPE
MLA_DOWN_PAD = 768

LANES = 128
HALO_ROWS = 8
VMEM_LIMIT = 56 * 1024 * 1024

TILES = dict(ffn=512, ffn_sub=256, tok=1024, tok_sub=256, tq=4096, tq_mla=4096, tq_general=1024, tkc=512, unroll=8)


def _tile(name, n):
    t = min(TILES[name], n)
    assert n % t == 0, (name, n, t)
    return t


def _params(*sem):
    return pltpu.CompilerParams(dimension_semantics=sem, vmem_limit_bytes=VMEM_LIMIT)


def _const_spec(shape):
    nd = len(shape)
    return pl.BlockSpec(shape, lambda *_: (0,) * nd, pipeline_mode=pl.Buffered(1))


def _layer_spec(stack_shape, layer):
    nd = len(stack_shape) - 1
    return pl.BlockSpec((None,) + tuple(stack_shape[1:]), lambda *_: (layer,) + (0,) * nd,
                        pipeline_mode=pl.Buffered(1))


def _rms(x, g):
    ms = jnp.mean(x * x, axis=-1, keepdims=True)
    return x * lax.rsqrt(ms + NORM_EPS) * g


def _dot(a, b):
    return jnp.dot(a, b, preferred_element_type=F32)


def _ffn_kernel(*refs, n_mix):
    x_ref, g_ref, win_ref, wout_ref = refs[:4]
    o_ref = refs[-1]
    tm = x_ref.shape[0]
    sub = _tile('ffn_sub', tm)
    subs = [slice(r0, r0 + sub) for r0 in range(0, tm, sub)]

    def mixed(rows):
        x = x_ref[rows, :]
        if n_mix:
            wmix_ref = refs[4 + n_mix]
            row = 0
            for a_ref in refs[4:4 + n_mix]:
                x = x + _dot(a_ref[rows, :], wmix_ref[row:row + a_ref.shape[1], :])
                row += a_ref.shape[1]
        return x

    xs = [mixed(rows) for rows in subs]
    hs = [_dot(_rms(x, g_ref[...]).astype(BF16), win_ref[...]) for x in xs]
    for rows, x, h in zip(subs, xs, hs):
        gate, up = h[:, :D_FF], h[:, D_FF:]
        act = (gate / (1.0 + jnp.exp(-gate)) * up).astype(BF16)
        o_ref[rows, :] = x + 0.5 * _dot(act, wout_ref[...])


def _ffn(x, g, w_in, w_out, layer, mix_acts=(), w_mix=None):
    t, d = x.shape
    tm = _tile('ffn', t)
    row = lambda wd: pl.BlockSpec((tm, wd), lambda i: (i, 0))
    mix_args = list(mix_acts) + ([w_mix] if mix_acts else [])
    mix_specs = [row(a.shape[1]) for a in mix_acts] + ([_const_spec(w_mix.shape)] if mix_acts else [])
    return pl.pallas_call(
        functools.partial(_ffn_kernel, n_mix=len(mix_acts)),
        grid=(t // tm,),
        in_specs=[row(d), _const_spec((1, d)), _layer_spec(w_in.shape, layer), _layer_spec(w_out.shape, layer)]
                 + mix_specs,
        out_specs=row(d),
        out_shape=jax.ShapeDtypeStruct((t, d), F32),
        compiler_params=_params("parallel"),
        name="mix_ffn" if mix_acts else "ffn",
    )(x, g, w_in, w_out, *mix_args)


def _seg64_rsqrt(x):
    n = x.shape[0]
    lo = lax.broadcasted_iota(jnp.int32, (n, LANES), 1) < ROPE_DIM
    outs = []
    for p in range(x.shape[1] // LANES):
        x2 = x[:, LANES * p:LANES * (p + 1)]
        x2 = x2 * x2
        s_lo = jnp.sum(jnp.where(lo, x2, 0.0), axis=-1, keepdims=True)
        s_hi = jnp.sum(jnp.where(lo, 0.0, x2), axis=-1, keepdims=True)
        outs.append(lax.rsqrt(jnp.where(lo, s_lo, s_hi) * (1.0 / ROPE_DIM) + NORM_EPS))
    return jnp.concatenate(outs, axis=-1)


def _rope_lanes(x, cos, sa, sb):
    w = x.shape[1]
    rep = w // LANES
    if rep > 1:
        cos, sa, sb = (jnp.concatenate([t] * rep, axis=1) for t in (cos, sa, sb))
    return x * cos + pltpu.roll(x, w - ROPE_HALF, 1) * sa + pltpu.roll(x, ROPE_HALF, 1) * sb


def _rope_tables(s):
    d = ROPE_DIM
    inv = 1.0 / (ROPE_THETA ** (jnp.arange(0, d, 2, dtype=F32) / d))
    pos = jnp.arange(s, dtype=F32)
    ang = pos[:, None] * jnp.tile(inv, LANES // ROPE_HALF)[None, :]
    first = (jnp.arange(LANES) % d < d // 2)[None, :]
    sin = jnp.sin(ang)
    sa = jnp.where(first, -sin, 0.0)
    sb = jnp.where(first, 0.0, sin)
    ang_t = inv[:, None] * pos[None, :]
    return (jnp.cos(ang), sa, sb), (jnp.cos(ang_t), jnp.sin(ang_t))


def _dot_nt(a, b):
    return lax.dot_general(a, b, (((1,), (1,)), ((), ())), preferred_element_type=F32)


def _rope_rows(x, cos_t, sin_t):
    x1, x2 = x[:ROPE_HALF], x[ROPE_HALF:]
    return x1 * cos_t - x2 * sin_t, x1 * sin_t + x2 * cos_t


def _even_in_kernel(x_ref, xp_ref, xn_ref, g_ref, w_ref, wqv_ref, cw_ref, qg_ref, kg_ref,
                    cos_ref, sa_ref, sb_ref, cost_ref, sint_ref, ya_ref, qT_ref, k_ref, vT_ref):
    i = pl.program_id(1)
    n = pl.num_programs(1)
    ts = x_ref.shape[0]
    c1, c2, c3 = CONV_WIDTH, 2 * CONV_WIDTH, 3 * CONV_WIDTH
    dh = DIFF_HEAD_DIM
    g = g_ref[...]
    cw = cw_ref[...]
    sub = _tile('tok_sub', ts)
    subs = [slice(r0, r0 + sub) for r0 in range(0, ts, sub)]

    def project(rows):
        xn = _rms(x_ref[rows, :], g).astype(BF16)
        return _dot(xn, w_ref[...]), _dot_nt(wqv_ref[...], xn)

    def finish(rows, proj, qv, z_before, z, z_after):
        vT_ref[:, rows] = qv[DIFF_WIDTH:].astype(BF16)
        cos_t, sin_t = cost_ref[:, rows], sint_ref[:, rows]
        for j in range(DIFF_WIDTH // dh):
            blk = qv[j * dh:(j + 1) * dh]
            r = lax.rsqrt(jnp.sum(blk * blk, axis=0, keepdims=True) * (1.0 / dh) + NORM_EPS)
            o1, o2 = _rope_rows(blk * r * qg_ref[j * dh:(j + 1) * dh, rows], cos_t, sin_t)
            qT_ref[j * dh:j * dh + dh // 2, rows] = o1.astype(BF16)
            qT_ref[j * dh + dh // 2:(j + 1) * dh, rows] = o2.astype(BF16)
        row = lax.broadcasted_iota(jnp.int32, (sub, c1), 0)
        z_m1 = jnp.where(row == 0, z_before, pltpu.roll(z, 1, 0))
        z_p1 = jnp.where(row == sub - 1, z_after, pltpu.roll(z, sub - 1, 0))
        conv = cw[0:1] * z_m1 + cw[1:2] * z + cw[2:3] * z_p1
        ya_ref[rows, :] = (proj[:, :c1] * conv).astype(BF16)
        k = proj[:, c3:]
        k_ref[rows, :] = _rope_lanes(k * _seg64_rsqrt(k) * kg_ref[...],
                                     cos_ref[rows, :], sa_ref[rows, :], sb_ref[rows, :]).astype(BF16)

    xh = jnp.concatenate([xp_ref[...], xn_ref[...]], axis=0)
    ph = _dot(_rms(xh, g).astype(BF16), w_ref[:, c1:c3])
    zh = ph[:, :c1] * ph[:, c1:]
    z_first = jnp.where(i > 0, zh[HALO_ROWS - 1:HALO_ROWS], 0.0)
    z_last = jnp.where(i < n - 1, zh[HALO_ROWS:HALO_ROWS + 1], 0.0)

    projs = [project(rows) for rows in subs]
    zs = [proj[:, c1:c2] * proj[:, c2:c3] for proj, _ in projs]
    for t, (rows, (proj, qv)) in enumerate(zip(subs, projs)):
        z_before = zs[t - 1][sub - 1:sub] if t > 0 else z_first
        z_after = zs[t + 1][0:1] if t + 1 < len(subs) else z_last
        finish(rows, proj, qv, z_before, zs[t], z_after)


def _even_in(x, g, w_tok, w_qv, conv_w, qg_t, kg, tables, tables_t):
    b, s, d = x.shape
    ts = _tile('tok', s)
    hb = ts // HALO_ROWS
    nhb = s // HALO_ROWS
    tok = lambda w: pl.BlockSpec((None, ts, w), lambda bi, i: (bi, i, 0))
    tokT = lambda w: pl.BlockSpec((None, w, ts), lambda bi, i: (bi, 0, i))
    tab = pl.BlockSpec((ts, LANES), lambda bi, i: (i, 0))
    tab_t = pl.BlockSpec((ROPE_HALF, ts), lambda bi, i: (0, i))
    qg_t = jnp.broadcast_to(qg_t, (DIFF_WIDTH, ts))
    return pl.pallas_call(
        _even_in_kernel,
        grid=(b, s // ts),
        in_specs=[tok(d),
                  pl.BlockSpec((None, HALO_ROWS, d), lambda bi, i: (bi, jnp.maximum(i * hb - 1, 0), 0)),
                  pl.BlockSpec((None, HALO_ROWS, d), lambda bi, i: (bi, jnp.minimum((i + 1) * hb, nhb - 1), 0)),
                  _const_spec((1, d)), _const_spec(w_tok.shape), _const_spec(w_qv.shape), _const_spec(conv_w.shape),
                  _const_spec(qg_t.shape), _const_spec(kg.shape), tab, tab, tab, tab_t, tab_t],
        out_specs=[tok(CONV_WIDTH), tokT(DIFF_WIDTH), tok(DIFF_WIDTH), tokT(DIFF_WIDTH)],
        out_shape=[jax.ShapeDtypeStruct((b, s, CONV_WIDTH), BF16),
                   jax.ShapeDtypeStruct((b, DIFF_WIDTH, s), BF16),
                   jax.ShapeDtypeStruct((b, s, DIFF_WIDTH), BF16),
                   jax.ShapeDtypeStruct((b, DIFF_WIDTH, s), BF16)],
        compiler_params=_params("parallel", "parallel"),
        name="even_in",
    )(x, x, x, g, w_tok, w_qv, conv_w, qg_t, kg, *tables, *tables_t)


def _flash_sweep(qmat, k_ref, vT_ref, tkc):
    nk = k_ref.shape[0] // tkc
    w = qmat.shape[1]
    dv = vT_ref.shape[0]

    def scores(c):
        off = pl.multiple_of(c * tkc, tkc)
        return _dot(k_ref[pl.ds(off, tkc), :], qmat)

    def consume(c, m, l, acc, s):
        off = pl.multiple_of(c * tkc, tkc)
        m_new = jnp.maximum(m, jnp.max(s, axis=0, keepdims=True))
        alpha = jnp.exp2(m - m_new)
        p = jnp.exp2(s - m_new)
        l = alpha * l + jnp.sum(p, axis=0, keepdims=True)
        acc = alpha * acc + _dot(vT_ref[:, pl.ds(off, tkc)], p.astype(BF16))
        return m_new, l, acc

    def step(c, carry):
        m, l, acc, s = carry
        s_next = scores(c + 1)
        return consume(c, m, l, acc, s) + (s_next,)

    init = (jnp.full((1, w), -jnp.inf, F32), jnp.zeros((1, w), F32), jnp.zeros((dv, w), F32), scores(0))
    m, l, acc, s = lax.fori_loop(0, nk - 1, step, init)
    _, l, acc = consume(nk - 1, m, l, acc, s)
    return l, acc


def _bounded_sweep(qmat, k_ref, vT_ref, l_ref, acc_ref, tkc):
    nk = k_ref.shape[0] // tkc
    l_ref[...] = jnp.zeros(l_ref.shape, F32)
    acc_ref[...] = jnp.zeros(acc_ref.shape, F32)

    def step(c, carry):
        off = pl.multiple_of(c * tkc, tkc)
        p = jnp.exp2(_dot(k_ref[pl.ds(off, tkc), :], qmat))
        l_ref[...] += jnp.sum(p, axis=0, keepdims=True)
        acc_ref[...] += _dot(vT_ref[:, pl.ds(off, tkc)], p.astype(BF16))
        return carry

    lax.fori_loop(0, nk, step, 0, unroll=min(nk, TILES['unroll']))
    return l_ref[...], acc_ref[...]


def _sweep(bounded, qmat, k_ref, vT_ref, l_ref, acc_ref, tkc):
    if bounded:
        return _bounded_sweep(qmat, k_ref, vT_ref, l_ref, acc_ref, tkc)
    return _flash_sweep(qmat, k_ref, vT_ref, tkc)


def _score_bound(d, scale, qg, kg):
    margin = 1.02
    return margin * scale * LOG2E * d * jnp.max(jnp.abs(qg)) * jnp.max(jnp.abs(kg))


def _diff_attn_kernel(qT_ref, k_ref, vT_ref, lam_ref, sg_ref, o_ref, qbd_ref, l_ref, acc_ref,
                      *, tkc, lambda_init, bounded):
    tq = qT_ref.shape[1]
    dh = DIFF_HEAD_DIM
    qbd_ref[...] = jnp.zeros(qbd_ref.shape, BF16)
    qbd_ref[0:dh, 0:tq] = qT_ref[0:dh, :]
    qbd_ref[dh:2 * dh, tq:2 * tq] = qT_ref[dh:2 * dh, :]
    l, acc = _sweep(bounded, qbd_ref[...], k_ref, vT_ref, l_ref, acc_ref, tkc)
    o = acc / l
    lv = lam_ref[...]
    lam = (jnp.exp(jnp.sum(lv[0:1] * lv[1:2], axis=-1, keepdims=True))
           - jnp.exp(jnp.sum(lv[2:3] * lv[3:4], axis=-1, keepdims=True)) + lambda_init)
    od = o[:, :tq] - lam * o[:, tq:]
    ms = jnp.mean(od * od, axis=0, keepdims=True)
    on = od * lax.rsqrt(ms + NORM_EPS) * sg_ref[...] * (1.0 - lambda_init)
    o_ref[...] = on.T.astype(o_ref.dtype)


def _diff_attn(qT, k, vT, lam_vecs, subln_g, lambda_init, bounded):
    b, w, s = qT.shape
    tq, tkc = _tile('tq' if bounded else 'tq_general', s), _tile('tkc', s)
    hw = 2 * DIFF_HEAD_DIM
    return pl.pallas_call(
        functools.partial(_diff_attn_kernel, tkc=tkc, lambda_init=lambda_init, bounded=bounded),
        grid=(b, DIFF_HEADS, s // tq),
        in_specs=[pl.BlockSpec((None, hw, tq), lambda bi, h, i: (bi, h, i)),
                  pl.BlockSpec((None, s, hw), lambda bi, h, i: (bi, 0, h)),
                  pl.BlockSpec((None, hw, s), lambda bi, h, i: (bi, h, 0)),
                  _const_spec(lam_vecs.shape), _const_spec(subln_g.shape)],
        out_specs=pl.BlockSpec((None, tq, hw), lambda bi, h, i: (bi, i, h)),
        out_shape=jax.ShapeDtypeStruct((b, s, w), BF16),
        scratch_shapes=[pltpu.VMEM((hw, 2 * tq), BF16), pltpu.VMEM((1, 2 * tq), F32),
                        pltpu.VMEM((hw, 2 * tq), F32)],
        compiler_params=_params("parallel", "parallel", "arbitrary"),
        name="diff_attn_bounded" if bounded else "diff_attn",
    )(qT, k, vT, lam_vecs, subln_g)


def _mla_attn_kernel(qT_ref, k_ref, vT_ref, o_ref, l_ref, acc_ref, *, tkc, bounded):
    l, acc = _sweep(bounded, qT_ref[...], k_ref, vT_ref, l_ref, acc_ref, tkc)
    o_ref[...] = (acc / l).T.astype(o_ref.dtype)


def _mla_attn(qT, k, vT, bounded):
    b, nh, dk, s = qT.shape
    tq, tkc = _tile('tq_mla' if bounded else 'tq_general', s), _tile('tkc', s)
    return pl.pallas_call(
        functools.partial(_mla_attn_kernel, tkc=tkc, bounded=bounded),
        grid=(b, nh, s // tq),
        in_specs=[pl.BlockSpec((None, None, dk, tq), lambda bi, h, i: (bi, h, 0, i)),
                  pl.BlockSpec((None, None, s, dk), lambda bi, h, i: (bi, h, 0, 0)),
                  pl.BlockSpec((None, MLA_V, s), lambda bi, h, i: (bi, h, 0))],
        out_specs=pl.BlockSpec((None, tq, MLA_V), lambda bi, h, i: (bi, i, h)),
        out_shape=jax.ShapeDtypeStruct((b, s, nh * MLA_V), BF16),
        scratch_shapes=[pltpu.VMEM((1, tq), F32), pltpu.VMEM((MLA_V, tq), F32)],
        compiler_params=_params("parallel", "parallel", "arbitrary"),
        name="mla_attn_bounded" if bounded else "mla_attn",
    )(qT, k, vT)


def _mla_in_kernel(x_ref, g_ref, wd_ref, qlg_ref, kvlg_ref, wuq_ref, wuk_ref, wuv_ref, qg_ref, kgn_ref, kgr_ref,
                   cos_ref, sa_ref, sb_ref, cost_ref, sint_ref, qT_ref, k_ref, vT_ref):
    nh, dn, dr = MLA_HEADS, MLA_NOPE, MLA_ROPE
    inv_d = 1.0 / MLA_QK
    ts = x_ref.shape[0]
    sub = _tile('tok_sub', ts)
    subs = [slice(r0, r0 + sub) for r0 in range(0, ts, sub)]

    def ssq(a):
        return jnp.sum(a * a, axis=-1, keepdims=True)

    def down(rows):
        xn = _rms(x_ref[rows, :], g_ref[...]).astype(BF16)
        return _dot(xn, wd_ref[...])

    def up(lat):
        cq = _rms(lat[:, :MLA_Q_RANK], qlg_ref[...]).astype(BF16)
        ckv = _rms(lat[:, MLA_Q_RANK:MLA_Q_RANK + MLA_KV_RANK], kvlg_ref[...]).astype(BF16)
        return _dot_nt(wuq_ref[...], cq), _dot_nt(wuv_ref[...], ckv), _dot(ckv, wuk_ref[...])

    def finish(rows, lat, qT, vT, kn):
        vT_ref[:, rows] = vT.astype(BF16)
        qg, cos_t, sin_t = qg_ref[:, rows], cost_ref[:, rows], sint_ref[:, rows]
        for h in range(nh):
            blk = qT[h * MLA_QK:(h + 1) * MLA_QK]
            r = lax.rsqrt(jnp.sum(blk * blk, axis=0, keepdims=True) * inv_d + NORM_EPS)
            blk = blk * r * qg
            o1, o2 = _rope_rows(blk[dn:], cos_t, sin_t)
            qT_ref[h, 0:dn, rows] = blk[:dn].astype(BF16)
            qT_ref[h, dn:dn + dr // 2, rows] = o1.astype(BF16)
            qT_ref[h, dn + dr // 2:, rows] = o2.astype(BF16)
        kr = lat[:, MLA_Q_RANK + MLA_KV_RANK:]
        kr_ssq = ssq(kr)
        kr_rot = _rope_lanes(kr * kgr_ref[...], cos_ref[rows, :], sa_ref[rows, :], sb_ref[rows, :])
        kgn = kgn_ref[...]
        for h in range(nh):
            kn_h = kn[:, h * dn:(h + 1) * dn]
            r = lax.rsqrt((ssq(kn_h) + kr_ssq) * inv_d + NORM_EPS)
            k_ref[h, rows, 0:dn] = (kn_h * r * kgn).astype(BF16)
            k_ref[h, rows, dn:dn + dr] = (kr_rot * r)[:, :dr].astype(BF16)

    lats = [down(rows) for rows in subs]
    ups = [up(lat) for lat in lats]
    for rows, lat, (qT, vT, kn) in zip(subs, lats, ups):
        finish(rows, lat, qT, vT, kn)


def _mla_in(x, g, wd, qlg, kvlg, wuq_t, wuk, wuv_t, qg_t, kgn, kgr, tables, tables_t):
    b, s, d = x.shape
    ts = _tile('tok', s)
    nh = MLA_HEADS
    tab = pl.BlockSpec((ts, LANES), lambda bi, i: (i, 0))
    tab_t = pl.BlockSpec((ROPE_HALF, ts), lambda bi, i: (0, i))
    consts = [g, wd, qlg, kvlg, wuq_t, wuk, wuv_t, jnp.broadcast_to(qg_t, (MLA_QK, ts)), kgn, kgr]
    return pl.pallas_call(
        _mla_in_kernel,
        grid=(b, s // ts),
        in_specs=[pl.BlockSpec((None, ts, d), lambda bi, i: (bi, i, 0))]
                 + [_const_spec(c.shape) for c in consts] + [tab, tab, tab, tab_t, tab_t],
        out_specs=[pl.BlockSpec((None, nh, MLA_QK, ts), lambda bi, i: (bi, 0, 0, i)),
                   pl.BlockSpec((None, nh, ts, MLA_QK), lambda bi, i: (bi, 0, i, 0)),
                   pl.BlockSpec((None, nh * MLA_V, ts), lambda bi, i: (bi, 0, i))],
        out_shape=[jax.ShapeDtypeStruct((b, nh, MLA_QK, s), BF16),
                   jax.ShapeDtypeStruct((b, nh, s, MLA_QK), BF16),
                   jax.ShapeDtypeStruct((b, nh * MLA_V, s), BF16)],
        compiler_params=_params("parallel", "parallel"),
        name="mla_in",
    )(x, *consts, *tables, *tables_t)


def _prep_weights(ffn1_norm, ffn1_w_in, ffn1_w_out, mix_norm, ffn2_norm, ffn2_w_in, ffn2_w_out,
                  even_w_in, even_conv_w, even_q_norm, even_k_norm, even_lambda, even_subln, even_w_out,
                  mla_w_down, mla_q_lat_norm, mla_kv_lat_norm, mla_w_uq, mla_w_ukv, mla_q_norm, mla_k_norm, mla_w_o):
    nh, dn, dr = MLA_HEADS, MLA_NOPE, MLA_ROPE
    c3 = 3 * CONV_WIDTH
    row = lambda a: a[:, None, :]
    col = lambda a: a[:, :, None]
    n_maps = DIFF_WIDTH // DIFF_HEAD_DIM

    def layers(w, f=lambda a: a):
        return [f(w[l]).astype(BF16) for l in range(w.shape[0])]

    def split_ukv(w):
        return w.reshape(MLA_KV_RANK, nh, dn + MLA_V)

    return dict(
        ffn1_norm=row(ffn1_norm), ffn1_w_in=ffn1_w_in.astype(BF16), ffn1_w_out=ffn1_w_out.astype(BF16),
        mix_norm=row(mix_norm),
        ffn2_norm=row(ffn2_norm), ffn2_w_in=ffn2_w_in.astype(BF16), ffn2_w_out=ffn2_w_out.astype(BF16),
        even_w_tok=layers(even_w_in, lambda w: jnp.concatenate(
            [w[:, :c3], w[:, c3 + DIFF_WIDTH:c3 + 2 * DIFF_WIDTH]], axis=-1)),
        even_w_qv=layers(even_w_in, lambda w: jnp.concatenate(
            [w[:, c3:c3 + DIFF_WIDTH], w[:, c3 + 2 * DIFF_WIDTH:]], axis=-1).T),
        even_conv_w=even_conv_w,
        even_qg_t=col(jnp.tile(even_q_norm, (1, n_maps)) * (DIFF_HEAD_DIM ** -0.5 * LOG2E)),
        even_kg=row(jnp.tile(even_k_norm, (1, n_maps))),
        even_lambda=even_lambda, even_subln=col(even_subln), even_w_out=layers(even_w_out),
        mla_w_down=layers(mla_w_down, lambda w: jnp.pad(w, ((0, 0), (0, MLA_DOWN_PAD - MLA_DOWN)))),
        mla_qlg=row(mla_q_lat_norm), mla_kvlg=row(mla_kv_lat_norm),
        mla_w_uq_t=layers(mla_w_uq, lambda w: w.T),
        mla_w_uk=layers(mla_w_ukv, lambda w: split_ukv(w)[..., :dn].reshape(MLA_KV_RANK, nh * dn)),
        mla_w_uv_t=layers(mla_w_ukv, lambda w: split_ukv(w)[..., dn:].reshape(MLA_KV_RANK, nh * MLA_V).T),
        mla_qg_t=col(mla_q_norm * (MLA_QK ** -0.5 * LOG2E)),
        mla_kgn=row(mla_k_norm[:, :dn]),
        mla_kgr=row(jnp.pad(mla_k_norm[:, dn:], ((0, 0), (0, LANES - dr)))),
        mla_w_o=layers(mla_w_o),
        even_bound=jax.vmap(functools.partial(_score_bound, DIFF_HEAD_DIM, DIFF_HEAD_DIM ** -0.5))(
            even_q_norm, even_k_norm),
        mla_bound=jax.vmap(functools.partial(_score_bound, MLA_QK, MLA_QK ** -0.5))(mla_q_norm, mla_k_norm),
    )


def _trunk(x, p):
    b, s, d = x.shape
    depth = p['ffn1_norm'].shape[0]
    tables, tables_t = _rope_tables(s)
    flat = lambda a: a.reshape(b * s, a.shape[-1])
    x = flat(x)
    for l in range(depth):
        i = l // 2
        x = _ffn(x, p['ffn1_norm'][l], p['ffn1_w_in'], p['ffn1_w_out'], l)
        x3 = x.reshape(b, s, d)
        if l % 2 == 0:
            lambda_init = 0.8 - 0.6 * math.exp(-0.3 * l)
            ya, qT, k, vT = _even_in(x3, p['mix_norm'][l], p['even_w_tok'][i], p['even_w_qv'][i],
                                     p['even_conv_w'][i], p['even_qg_t'][i], p['even_kg'][i], tables, tables_t)
            yb = lax.cond(
                p['even_bound'][i] <= SCORE_BOUND,
                functools.partial(_diff_attn, lambda_init=lambda_init, bounded=True),
                functools.partial(_diff_attn, lambda_init=lambda_init, bounded=False),
                qT, k, vT, p['even_lambda'][i], p['even_subln'][i])
            mix_acts, w_mix = (flat(ya), flat(yb)), p['even_w_out'][i]
        else:
            qT, k, vT = _mla_in(x3, p['mix_norm'][l], p['mla_w_down'][i], p['mla_qlg'][i], p['mla_kvlg'][i],
                                p['mla_w_uq_t'][i], p['mla_w_uk'][i], p['mla_w_uv_t'][i], p['mla_qg_t'][i],
                                p['mla_kgn'][i], p['mla_kgr'][i], tables, tables_t)
            o = lax.cond(p['mla_bound'][i] <= SCORE_BOUND,
                         functools.partial(_mla_attn, bounded=True),
                         functools.partial(_mla_attn, bounded=False), qT, k, vT)
            mix_acts, w_mix = (flat(o),), p['mla_w_o'][i]
        x = _ffn(x, p['ffn2_norm'][l], p['ffn2_w_in'], p['ffn2_w_out'], l, mix_acts, w_mix)
    return x.reshape(b, s, d)


def kernel(x_prompt, x_sample, ffn1_norm, ffn1_w_in, ffn1_w_out, mix_norm, ffn2_norm, ffn2_w_in, ffn2_w_out, even_w_in, even_conv_w, even_q_norm, even_k_norm, even_lambda, even_subln, even_w_out, mla_w_down, mla_q_lat_norm, mla_kv_lat_norm, mla_w_uq, mla_w_ukv, mla_q_norm, mla_k_norm, mla_w_o):
    p = _prep_weights(ffn1_norm, ffn1_w_in, ffn1_w_out, mix_norm, ffn2_norm, ffn2_w_in, ffn2_w_out,
                      even_w_in, even_conv_w, even_q_norm, even_k_norm, even_lambda, even_subln, even_w_out,
                      mla_w_down, mla_q_lat_norm, mla_kv_lat_norm, mla_w_uq, mla_w_ukv, mla_q_norm, mla_k_norm,
                      mla_w_o)
    return (_trunk(x_prompt, p), _trunk(x_sample, p))
```
